```python
import jax, jax.numpy as jnp
from jax import lax
import numpy as np

D_MODEL = 1024
BATCH = 8
SEQ = 8192
DEPTH = 1

N_META = 16
D_RNN = D_MODEL
N_RNN_BLOCKS = 4
RNN_BLOCK = D_RNN // N_RNN_BLOCKS
CONV_WIDTH = 4
LRU_C = 8.0
POOL_WINDOWS = (2, 4, 8, 16)
N_POOL_GROUPS = len(POOL_WINDOWS)
D_POOL = D_MODEL // 2
POOL_GROUP = D_POOL // N_POOL_GROUPS
N_BRANCH = 2
D_IN_PROJ = 2 * D_RNN + D_POOL + N_BRANCH * D_MODEL
PEER_HEADS = 8
PEER_NKEYS = 128
N_EXPERTS = PEER_NKEYS * PEER_NKEYS
PEER_DK = 256
PEER_HALF = PEER_DK // 2
PEER_TOPK = 16
PEER_TOKENS_PER_STEP = 256
ALPHA = (2.0 * DEPTH) ** 0.25
BETA = (8.0 * DEPTH) ** -0.25
LN_EPS = 1e-5

kernel_name = "hybrid_rglru_pool_peer_deepnorm"


def layer_norm(x, g, b):
    xf = x.astype(jnp.float32)
    mu = jnp.mean(xf, axis=-1, keepdims=True)
    xc = xf - mu
    var = jnp.mean(jnp.square(xc), axis=-1, keepdims=True)
    return xc * lax.rsqrt(var + LN_EPS) * g.astype(jnp.float32) + b.astype(jnp.float32)


def _lru_combine(earlier, later):
    a1, b1 = earlier
    a2, b2 = later
    return a1 * a2, a2 * b1 + b2


def rglru_branch(xr, gate, conv_w, conv_b, w_rg, b_rg, w_ig, b_ig, lru_L):
    B, T, _ = xr.shape
    xc = lax.conv_general_dilated(
        xr.astype(jnp.float32), conv_w.astype(jnp.float32)[:, None, :],
        window_strides=(1,), padding=[(CONV_WIDTH - 1, 0)],
        dimension_numbers=("NWC", "WIO", "NWC"), feature_group_count=D_RNN,
    ) + conv_b
    xb = xc.reshape(B, T, N_RNN_BLOCKS, RNN_BLOCK)
    r = jax.nn.sigmoid(jnp.einsum("btnc,ncd->btnd", xb, w_rg) + b_rg).reshape(B, T, D_RNN)
    i = jax.nn.sigmoid(jnp.einsum("btnc,ncd->btnd", xb, w_ig) + b_ig).reshape(B, T, D_RNN)
    log_a = LRU_C * r * jax.nn.log_sigmoid(lru_L.astype(jnp.float32))
    a = jnp.exp(log_a)
    mult = jnp.sqrt(-jnp.expm1(2.0 * log_a))
    _, h = lax.associative_scan(_lru_combine, (a, mult * i * xc), axis=1)
    return h * jax.nn.gelu(gate.astype(jnp.float32))


def pool_branch(xp, pool_w, pool_scale):
    B, T, _ = xp.shape
    xf = xp.astype(jnp.float32)
    c = jnp.cumsum(xf, axis=1)
    count = jnp.arange(1, T + 1, dtype=jnp.float32)[None, :, None]
    means = []
    for g, w in enumerate(POOL_WINDOWS):
        cg = c[..., g * POOL_GROUP:(g + 1) * POOL_GROUP]
        lag = jnp.pad(cg[:, :-w], ((0, 0), (w, 0), (0, 0)))
        means.append((cg - lag) / jnp.minimum(count, float(w)))
    pooled = jnp.concatenate(means, axis=-1) - xf
    mixed = jnp.einsum("btgc,gcd->btgd", pooled.reshape(B, T, N_POOL_GROUPS, POOL_GROUP), pool_w)
    return mixed.reshape(B, T, D_POOL) * pool_scale


def peer_retrieve(h, w_q, sub_keys):
    N = h.shape[0]
    q = jnp.einsum("nd,de->ne", h, w_q).reshape(N, PEER_HEADS, 2, PEER_HALF)
    s = jnp.einsum("nhpd,hpkd->nhpk", q, sub_keys).astype(jnp.float32)
    sv, si = lax.top_k(s, PEER_TOPK)
    comb = (sv[:, :, 0, :, None] + sv[:, :, 1, None, :]).reshape(N, PEER_HEADS, PEER_TOPK * PEER_TOPK)
    cv, ci = lax.top_k(comb, PEER_TOPK)
    k1 = jnp.take_along_axis(si[:, :, 0], ci // PEER_TOPK, axis=-1)
    k2 = jnp.take_along_axis(si[:, :, 1], ci % PEER_TOPK, axis=-1)
    ids = k1 * PEER_NKEYS + k2
    wts = jax.nn.softmax(cv, axis=-1)
    return ids.reshape(N, PEER_HEADS * PEER_TOPK), wts.reshape(N, PEER_HEADS * PEER_TOPK)


def peer_experts(h, ids, wts, expert_u, expert_v):
    N, D = h.shape
    n_pad = (-N) % PEER_TOKENS_PER_STEP
    hp = jnp.pad(h, ((0, n_pad), (0, 0))).reshape(-1, PEER_TOKENS_PER_STEP, D)
    ip = jnp.pad(ids, ((0, n_pad), (0, 0))).reshape(-1, PEER_TOKENS_PER_STEP, ids.shape[1])
    wp = jnp.pad(wts, ((0, n_pad), (0, 0))).reshape(-1, PEER_TOKENS_PER_STEP, wts.shape[1])

    def step(args):
        hc, ic, wc = args
        u = expert_u[ic]
        z = jax.nn.gelu(jnp.einsum("cd,ced->ce", hc, u).astype(jnp.float32))
        v = expert_v[ic]
        return jnp.einsum("ce,ced->cd", wc * z, v).astype(jnp.float32)

    out = lax.map(step, (hp, ip, wp))
    return out.reshape(-1, D)[:N]


def setup_inputs(seed: int = 0) -> dict:
    key = jax.random.key(seed)
    ks = jax.random.split(key, 26)
    f32 = jnp.float32
    nrm = lambda k, shape, scale: jax.random.normal(k, shape, f32) * scale
    L = DEPTH
    u_a = jax.random.uniform(ks[11], (L, D_RNN), f32, minval=0.9, maxval=0.999)
    a_base = u_a ** (1.0 / LRU_C)
    return {
        "x": nrm(ks[0], (BATCH, SEQ, D_MODEL), 1.0),
        "meta": nrm(ks[1], (N_META, D_MODEL), 1.0),
        "ln_in_g": 1.0 + nrm(ks[2], (D_MODEL,), 0.02),
        "ln_in_b": nrm(ks[3], (D_MODEL,), 0.02),
        "w_in": nrm(ks[4], (L, D_MODEL, D_IN_PROJ), D_MODEL ** -0.5),
        "conv_w": nrm(ks[5], (L, CONV_WIDTH, D_RNN), CONV_WIDTH ** -0.5),
        "conv_b": nrm(ks[6], (L, D_RNN), 0.01),
        "w_rg": nrm(ks[7], (L, N_RNN_BLOCKS, RNN_BLOCK, RNN_BLOCK), RNN_BLOCK ** -0.5),
        "b_rg": nrm(ks[8], (L, N_RNN_BLOCKS, RNN_BLOCK), 0.01),
        "w_ig": nrm(ks[9], (L, N_RNN_BLOCKS, RNN_BLOCK, RNN_BLOCK), RNN_BLOCK ** -0.5),
        "b_ig": nrm(ks[10], (L, N_RNN_BLOCKS, RNN_BLOCK), 0.01),
        "lru_L": jnp.log(a_base) - jnp.log1p(-a_base),
        "w_proj_a": nrm(ks[12], (L, D_RNN, D_MODEL), D_RNN ** -0.5),
        "pool_w": nrm(ks[13], (L, N_POOL_GROUPS, POOL_GROUP, POOL_GROUP), POOL_GROUP ** -0.5),
        "pool_scale": 1.0 + nrm(ks[14], (L, D_POOL), 0.02),
        "w_proj_b": nrm(ks[15], (L, D_POOL, D_MODEL), D_POOL ** -0.5),
        "w_out": nrm(ks[16], (L, D_MODEL, D_MODEL), BETA * D_MODEL ** -0.5),
        "ln1_g": 1.0 + nrm(ks[17], (L, D_MODEL), 0.02),
        "ln1_b": nrm(ks[18], (L, D_MODEL), 0.02),
        "w_q": nrm(ks[19], (L, D_MODEL, PEER_HEADS * PEER_DK), D_MODEL ** -0.5),
        "sub_keys": nrm(ks[20], (L, PEER_HEADS, 2, PEER_NKEYS, PEER_HALF), PEER_HALF ** -0.5),
        "expert_u": nrm(ks[21], (L, N_EXPERTS, D_MODEL), D_MODEL ** -0.5),
        "expert_v": nrm(ks[22], (L, N_EXPERTS, D_MODEL), BETA),
        "ln2_g": 1.0 + nrm(ks[23], (L, D_MODEL), 0.02),
        "ln2_b": nrm(ks[24], (L, D_MODEL), 0.02),
    }


def reference(x, meta, ln_in_g, ln_in_b, w_in, conv_w, conv_b, w_rg, b_rg, w_ig, b_ig, lru_L,
              w_proj_a, pool_w, pool_scale, w_proj_b, w_out, ln1_g, ln1_b, w_q, sub_keys,
              expert_u, expert_v, ln2_g, ln2_b):
    B = x.shape[0]
    meta_b = jnp.broadcast_to(meta[None].astype(x.dtype), (B, N_META, D_MODEL))
    h = layer_norm(jnp.concatenate([meta_b, x], axis=1), ln_in_g, ln_in_b)
    T = h.shape[1]
    for l in range(DEPTH):
        proj = jnp.einsum("btd,de->bte", h, w_in[l])
        xr = proj[..., :D_RNN]
        gate_r = proj[..., D_RNN:2 * D_RNN]
        xp = proj[..., 2 * D_RNN:2 * D_RNN + D_POOL]
        g_logits = proj[..., 2 * D_RNN + D_POOL:]
        y_a = jnp.einsum("bte,ed->btd",
                         rglru_branch(xr, gate_r, conv_w[l], conv_b[l], w_rg[l], b_rg[l],
                                      w_ig[l], b_ig[l], lru_L[l]), w_proj_a[l])
        y_b = jnp.einsum("bte,ed->btd", pool_branch(xp, pool_w[l], pool_scale[l]), w_proj_b[l])
        gates = jax.nn.sigmoid(g_logits.astype(jnp.float32))
        merged = gates[..., :D_MODEL] * y_a + gates[..., D_MODEL:] * y_b
        mix_out = jnp.einsum("btd,de->bte", merged, w_out[l])
        h = layer_norm(ALPHA * h + mix_out, ln1_g[l], ln1_b[l])
        hf = h.reshape(B * T, D_MODEL)
        ids, wts = peer_retrieve(hf, w_q[l], sub_keys[l])
        ffn = peer_experts(hf, ids, wts, expert_u[l], expert_v[l]).reshape(B, T, D_MODEL)
        h = layer_norm(ALPHA * h + ffn, ln2_g[l], ln2_b[l])
    return h[:, N_META:].astype(x.dtype)
```

```python
import functools

import jax
import jax.numpy as jnp
from jax import lax
from jax.experimental import pallas as pl
from jax.experimental.pallas import tpu as pltpu

D_MODEL = 1024
N_META = 16
D_RNN = D_MODEL
N_RNN_BLOCKS = 4
RNN_BLOCK = D_RNN // N_RNN_BLOCKS
CONV_WIDTH = 4
LRU_C = 8.0
POOL_WINDOWS = (2, 4, 8, 16)
D_POOL = D_MODEL // 2
POOL_GROUP = D_POOL // len(POOL_WINDOWS)
PEER_HEADS = 8
PEER_NKEYS = 128
N_EXPERTS = PEER_NKEYS * PEER_NKEYS
PEER_HALF = 128
PEER_TOPK = 16
DEPTH = 1
ALPHA = (2.0 * DEPTH) ** 0.25
LN_EPS = 1e-5

SUBLANES = 8
LANES = 128
CONV_TAIL = SUBLANES
POOL_TAIL = 16
VMEM_LIMIT = 56 * 1024 * 1024

_F32 = jnp.float32
_BF16 = jnp.bfloat16


def _layer_norm(v, g, b):
    mu = jnp.mean(v, axis=-1, keepdims=True)
    vc = v - mu
    var = jnp.mean(vc * vc, axis=-1, keepdims=True)
    return vc * lax.rsqrt(var + LN_EPS) * g + b


def _gelu_tanh(v):
    return 0.5 * v * (1.0 + jnp.tanh(0.7978845608028654 * (v + 0.044715 * (v * v * v))))


def _sigmoid(v):
    return 1.0 / (1.0 + jnp.exp(-v))


def _dot(a, b):
    return jnp.dot(a, b, preferred_element_type=_F32)


def _lru_scan(a, b, h_prev):
    tt, c = a.shape
    groups = tt // SUBLANES
    a3 = a.reshape(groups, SUBLANES, c)
    b3 = b.reshape(groups, SUBLANES, c)
    row = lax.broadcasted_iota(jnp.int32, a3.shape, 1)
    shift = 1
    while shift < SUBLANES:
        a_sh = pltpu.roll(a3, shift, axis=1)
        b_sh = pltpu.roll(b3, shift, axis=1)
        m = row >= shift
        b3 = jnp.where(m, a3 * b_sh + b3, b3)
        a3 = jnp.where(m, a3 * a_sh, a3)
        shift *= 2
    outs = []
    carry = h_prev
    for g in range(groups):
        hg = a3[g] * carry + b3[g]
        outs.append(hg)
        carry = hg[SUBLANES - 1:SUBLANES, :]
    return jnp.concatenate(outs, axis=0)


def _mixer_kernel(x_ref, lng_ref, lnb_ref, win_ref, convw_ref, convb_ref, wgate_ref, brg_ref,
                  big_ref, lrul_ref, wpa_ref, poolw_ref, pscale_ref, wpb_ref, wout_ref,
                  ln1g_ref, ln1b_ref, xr0_ref, xp0_ref, hs0_ref,
                  h1_ref, *rest, tt, pos0, emit_state):
    if emit_state:
        xr_out, xp_out, hs_out, xr_buf, xp_buf, hs_buf, rg_buf, mx_buf = rest
    else:
        xr_buf, xp_buf, hs_buf, rg_buf, mx_buf = rest
    t = pl.program_id(1)

    @pl.when(t == 0)
    def _():
        xr_buf[0:CONV_TAIL, :] = xr0_ref[...]
        xp_buf[0:POOL_TAIL, :] = xp0_ref[...]
        hs_buf[...] = hs0_ref[...]

    x = x_ref[0]
    h0 = _layer_norm(x, lng_ref[...], lnb_ref[...])
    h0b = h0.astype(_BF16)

    xr_buf[CONV_TAIL:CONV_TAIL + tt, :] = _dot(h0b, win_ref[:, 0:D_RNN])
    lsig = lrul_ref[...]
    log_sig = jnp.minimum(lsig, 0.0) - jnp.log1p(jnp.exp(-jnp.abs(lsig)))
    for n in range(N_RNN_BLOCKS):
        cs = slice(n * RNN_BLOCK, (n + 1) * RNN_BLOCK)
        xc = convb_ref[:, cs]
        for j in range(CONV_WIDTH):
            off = CONV_TAIL - (CONV_WIDTH - 1) + j
            xc = xc + convw_ref[j:j + 1, cs] * xr_buf[off:off + tt, cs]
        pre = _dot(xc.astype(_BF16), wgate_ref[n])
        r = _sigmoid(pre[:, :RNN_BLOCK] + brg_ref[:, cs])
        i = _sigmoid(pre[:, RNN_BLOCK:] + big_ref[:, cs])
        log_a = (LRU_C * log_sig[:, cs]) * r
        a = jnp.exp(log_a)
        mult = jnp.sqrt(1.0 - a * a)
        h = _lru_scan(a, mult * i * xc, hs_buf[0:1, cs])
        hs_buf[:, cs] = jnp.broadcast_to(h[tt - 1:tt, :], (SUBLANES, RNN_BLOCK))
        gate = _dot(h0b, win_ref[:, D_RNN + n * RNN_BLOCK:D_RNN + (n + 1) * RNN_BLOCK])
        rg_buf[:, cs] = (h * _gelu_tanh(gate)).astype(_BF16)
    y_a = _dot(rg_buf[...], wpa_ref[...])

    xp_buf[POOL_TAIL:POOL_TAIL + tt, :] = _dot(h0b, win_ref[:, 2 * D_RNN:2 * D_RNN + D_POOL])
    if pos0 < max(POOL_WINDOWS):
        pos = (lax.broadcasted_iota(jnp.int32, (tt, 1), 0) + (pos0 + 1) + t * tt).astype(_F32)
    for g, w in enumerate(POOL_WINDOWS):
        cs = slice(g * POOL_GROUP, (g + 1) * POOL_GROUP)
        cur = xp_buf[POOL_TAIL:POOL_TAIL + tt, cs]
        acc = cur
        for j in range(1, w):
            acc = acc + xp_buf[POOL_TAIL - j:POOL_TAIL - j + tt, cs]
        if pos0 < max(POOL_WINDOWS):
            mean = acc / jnp.minimum(pos, float(w))
        else:
            mean = acc * (1.0 / w)
        mixed = _dot((mean - cur).astype(_BF16), poolw_ref[g]) * pscale_ref[:, cs]
        mx_buf[:, cs] = mixed.astype(_BF16)
    y_b = _dot(mx_buf[...], wpb_ref[...])

    g_off = 2 * D_RNN + D_POOL
    gate_a = _sigmoid(_dot(h0b, win_ref[:, g_off:g_off + D_MODEL]))
    gate_b = _sigmoid(_dot(h0b, win_ref[:, g_off + D_MODEL:g_off + 2 * D_MODEL]))
    merged = gate_a * y_a + gate_b * y_b
    mix = _dot(merged.astype(_BF16), wout_ref[...])
    h1_ref[0] = _layer_norm(ALPHA * h0 + mix, ln1g_ref[...], ln1b_ref[...])

    xr_buf[0:CONV_TAIL, :] = xr_buf[tt:tt + CONV_TAIL, :]
    xp_buf[0:POOL_TAIL, :] = xp_buf[tt:tt + POOL_TAIL, :]
    if emit_state:
        xr_out[...] = xr_buf[0:CONV_TAIL, :]
        xp_out[...] = xp_buf[0:POOL_TAIL, :]
        hs_out[...] = hs_buf[...]


def _const_spec(shape):
    nd = len(shape)
    return pl.BlockSpec(shape, lambda b, t: (0,) * nd, pipeline_mode=pl.Buffered(1))


def _mixer(x, weights, state, *, tt, pos0, emit_state):
    B, S, D = x.shape
    assert S % tt == 0 and tt % POOL_TAIL == 0
    in_specs = [pl.BlockSpec((1, tt, D), lambda b, t: (b, t, 0))]
    in_specs += [_const_spec(w.shape) for w in weights]
    in_specs += [_const_spec(s.shape) for s in state]
    out_shape = [jax.ShapeDtypeStruct((B, S, D), _F32)]
    out_specs = [pl.BlockSpec((1, tt, D), lambda b, t: (b, t, 0))]
    if emit_state:
        assert B == 1 and S == tt
        out_shape += [jax.ShapeDtypeStruct(s.shape, _F32) for s in state]
        out_specs += [pl.BlockSpec(s.shape, lambda b, t: (0, 0)) for s in state]
    scratch = [
        pltpu.VMEM((CONV_TAIL + tt, D_RNN), _F32),
        pltpu.VMEM((POOL_TAIL + tt, D_POOL), _F32),
        pltpu.VMEM((SUBLANES, D_RNN), _F32),
        pltpu.VMEM((tt, D_RNN), _BF16),
        pltpu.VMEM((tt, D_POOL), _BF16),
    ]
    return pl.pallas_call(
        functools.partial(_mixer_kernel, tt=tt, pos0=pos0, emit_state=emit_state),
        grid=(B, S // tt),
        in_specs=in_specs,
        out_specs=out_specs,
        out_shape=out_shape,
        scratch_shapes=scratch,
        compiler_params=pltpu.CompilerParams(
            dimension_semantics=("arbitrary", "arbitrary"), vmem_limit_bytes=VMEM_LIMIT),
        name="mixer_meta" if emit_state else "mixer",
    )(x, *weights, *state)


def _oddeven_merge_sort_pairs(n):
    pairs = []

    def merge(lo, hi, r):
        step = r * 2
        if step < hi - lo:
            merge(lo, hi, step)
            merge(lo + r, hi, step)
            for i in range(lo + r, hi - r, step):
                pairs.append((i, i + r))
        else:
            pairs.append((lo, lo + r))

    def sort(lo, hi):
        if hi - lo >= 1:
            mid = lo + (hi - lo) // 2
            sort(lo, mid)
            sort(mid + 1, hi)
            merge(lo, hi, 1)

    sort(0, n - 1)
    return pairs


_SORT16 = _oddeven_merge_sort_pairs(PEER_TOPK)


def _cmpx(v, i, j):
    hi = jnp.maximum(v[i], v[j])
    lo = jnp.minimum(v[i], v[j])
    v[i] = hi
    v[j] = lo


def _sort16_desc(v):
    v = list(v)
    for i, j in _SORT16:
        _cmpx(v, i, j)
    return v


def _merge_top16(a, b):
    c = list(a)
    for k in range(len(b)):
        c[PEER_TOPK - 1 - k] = jnp.maximum(a[PEER_TOPK - 1 - k], b[k])
    stride = PEER_TOPK // 2
    while stride >= 1:
        for i in range(PEER_TOPK):
            if i & stride == 0:
                _cmpx(c, i, i + stride)
        stride //= 2
    return c


def _top16_desc(vals):
    runs = [_sort16_desc(vals[i:i + PEER_TOPK]) for i in range(0, len(vals), PEER_TOPK)]
    while len(runs) > 1:
        runs = [_merge_top16(runs[i], runs[i + 1]) for i in range(0, len(runs), 2)]
    return runs[0]


_CAND = [(i, j) for i in range(PEER_TOPK) for j in range(PEER_TOPK) if (i + 1) * (j + 1) <= PEER_TOPK]


def _retrieve_kernel(h1_ref, wq_ref, keys_ref, cnt1_ref, e1_ref, rho2_ref, e2_ref,
                     s_buf, x_buf, *, tn, chunk):
    hb = h1_ref[...].astype(_BF16)
    q_t = lax.dot_general(wq_ref[...], hb, (((1,), (1,)), ((), ())),
                          preferred_element_type=_F32).astype(_BF16)
    half_rows = PEER_HEADS * PEER_HALF
    for p in range(2):
        s_buf[p] = _dot(keys_ref[p], q_t[p * half_rows:(p + 1) * half_rows, :])

    neg = jnp.full((PEER_HEADS, chunk), -jnp.inf, _F32)

    def slab(k):
        return pl.ds(pl.multiple_of(k * PEER_HEADS, PEER_HEADS), PEER_HEADS)

    def chunk_body(c, _):
        ls = pl.ds(pl.multiple_of(c * chunk, chunk), chunk)
        a = _top16_desc([s_buf[0, k * PEER_HEADS:(k + 1) * PEER_HEADS, ls] for k in range(PEER_NKEYS)])
        b = _top16_desc([s_buf[1, k * PEER_HEADS:(k + 1) * PEER_HEADS, ls] for k in range(PEER_NKEYS)])
        rows = {}
        for i, j in _CAND:
            rows.setdefault(i, []).append(a[i] + b[j])
        top = rows[0]
        rest = [v for i in range(1, PEER_TOPK) for v in rows[i]]
        while rest:
            grp, rest = rest[:PEER_TOPK], rest[PEER_TOPK:]
            grp = grp + [neg] * (PEER_TOPK - len(grp))
            top = _merge_top16(top, _sort16_desc(grp))
        c16 = top[PEER_TOPK - 1]
        ea = [jnp.exp(a[i] - a[0]) for i in range(PEER_TOPK)]
        eb = [jnp.exp(b[j] - b[0]) for j in range(PEER_TOPK)]
        z = jnp.zeros((PEER_HEADS, chunk), _F32)
        for i, j in _CAND:
            z = z + jnp.where(a[i] + b[j] >= c16, ea[i] * eb[j], 0.0)
        zinv = 1.0 / z

        def key_body(k, _):
            s1 = s_buf[0, slab(k), ls]
            s2 = s_buf[1, slab(k), ls]
            cnt = jnp.zeros((PEER_HEADS, chunk), _F32)
            rho = jnp.zeros((PEER_HEADS, chunk), _F32)
            for j in range(PEER_TOPK):
                cnt = cnt + jnp.where(s1 + b[j] >= c16, 1.0, 0.0)
                rho = rho + jnp.where(b[j] > s2, 1.0, 0.0)
            x_buf[0, slab(k), :] = cnt
            x_buf[1, slab(k), :] = jnp.exp(s1 - a[0]) * zinv
            x_buf[2, slab(k), :] = rho
            x_buf[3, slab(k), :] = jnp.exp(s2 - b[0])
            return 0

        lax.fori_loop(0, PEER_NKEYS, key_body, 0)

        for idx, o_ref in enumerate((cnt1_ref, e1_ref, rho2_ref, e2_ref)):
            for h in range(PEER_HEADS):
                o_ref[h, :, ls] = x_buf[idx, pl.ds(h, PEER_NKEYS, stride=PEER_HEADS), :]
        return 0

    lax.fori_loop(0, tn // chunk, chunk_body, 0)


def _retrieve(h1, wq_t, keys_kh, *, tn, chunk=LANES):
    N = h1.shape[0]
    assert N % tn == 0 and tn % chunk == 0 and chunk == LANES
    out_sds = jax.ShapeDtypeStruct((PEER_HEADS, PEER_NKEYS, N), _F32)
    out_spec = pl.BlockSpec((PEER_HEADS, PEER_NKEYS, tn), lambda i: (0, 0, i))
    return pl.pallas_call(
        functools.partial(_retrieve_kernel, tn=tn, chunk=chunk),
        grid=(N // tn,),
        in_specs=[
            pl.BlockSpec((tn, D_MODEL), lambda i: (i, 0)),
            pl.BlockSpec(wq_t.shape, lambda i: (0, 0), pipeline_mode=pl.Buffered(1)),
            pl.BlockSpec(keys_kh.shape, lambda i: (0, 0, 0), pipeline_mode=pl.Buffered(1)),
        ],
        out_specs=[out_spec] * 4,
        out_shape=[out_sds] * 4,
        scratch_shapes=[
            pltpu.VMEM((2, PEER_NKEYS * PEER_HEADS, tn), _F32),
            pltpu.VMEM((4, PEER_NKEYS * PEER_HEADS, chunk), _F32),
        ],
        compiler_params=pltpu.CompilerParams(
            dimension_semantics=("arbitrary",), vmem_limit_bytes=VMEM_LIMIT),
        name="peer_retrieve",
    )(h1, wq_t, keys_kh)


def _experts_kernel(h1_ref, u_ref, vt_ref, cnt1_ref, e1_ref, rho2_ref, e2_ref, g_ref, b_ref,
                    out_ref, hb_buf, z_buf, gz_buf, acc_buf, *, tn, te, lane_chunk):
    e = pl.program_id(1)
    rows = te // PEER_NKEYS

    @pl.when(e == 0)
    def _():
        hb_buf[...] = h1_ref[...].astype(_BF16)
        acc_buf[...] = jnp.zeros_like(acc_buf)

    z_buf[...] = lax.dot_general(u_ref[...], hb_buf[...], (((1,), (1,)), ((), ())),
                                 preferred_element_type=_F32)
    k1_base = pl.multiple_of(e * rows, rows)
    for c in range(tn // lane_chunk):
        ls = slice(c * lane_chunk, (c + 1) * lane_chunk)
        cnt1 = [cnt1_ref[h, pl.ds(k1_base, rows), ls] for h in range(PEER_HEADS)]
        e1 = [e1_ref[h, pl.ds(k1_base, rows), ls] for h in range(PEER_HEADS)]
        for r in range(rows):
            g = jnp.zeros((PEER_NKEYS, lane_chunk), _F32)
            for h in range(PEER_HEADS):
                m = rho2_ref[h, :, ls] < cnt1[h][r:r + 1, :]
                g = g + jnp.where(m, e2_ref[h, :, ls], 0.0) * e1[h][r:r + 1, :]
            rs = slice(r * PEER_NKEYS, (r + 1) * PEER_NKEYS)
            gz_buf[rs, ls] = (g * _gelu_tanh(z_buf[rs, ls])).astype(_BF16)
    acc_buf[...] += _dot(vt_ref[...], gz_buf[...])

    @pl.when(e == pl.num_programs(1) - 1)
    def _():
        ffn = acc_buf[...].T
        out_ref[...] = _layer_norm(ALPHA * h1_ref[...] + ffn, g_ref[...], b_ref[...])


def _experts(h1, u_b, vt_b, cnt1, e1, rho2, e2, ln_g, ln_b, *, tn, te, lane_chunk=256):
    N = h1.shape[0]
    assert N % tn == 0 and N_EXPERTS % te == 0 and te % (SUBLANES * PEER_NKEYS) == 0
    sel_spec = pl.BlockSpec((PEER_HEADS, PEER_NKEYS, tn), lambda i, e: (0, 0, i))
    return pl.pallas_call(
        functools.partial(_experts_kernel, tn=tn, te=te, lane_chunk=min(lane_chunk, tn)),
        grid=(N // tn, N_EXPERTS // te),
        in_specs=[
            pl.BlockSpec((tn, D_MODEL), lambda i, e: (i, 0)),
            pl.BlockSpec((te, D_MODEL), lambda i, e: (e, 0)),
            pl.BlockSpec((D_MODEL, te), lambda i, e: (0, e)),
            sel_spec, sel_spec, sel_spec, sel_spec,
            pl.BlockSpec((1, D_MODEL), lambda i, e: (0, 0)),
            pl.BlockSpec((1, D_MODEL), lambda i, e: (0, 0)),
        ],
        out_specs=pl.BlockSpec((tn, D_MODEL), lambda i, e: (i, 0)),
        out_shape=jax.ShapeDtypeStruct((N, D_MODEL), _F32),
        scratch_shapes=[
            pltpu.VMEM((tn, D_MODEL), _BF16),
            pltpu.VMEM((te, tn), _F32),
            pltpu.VMEM((te, tn), _BF16),
            pltpu.VMEM((D_MODEL, tn), _F32),
        ],
        compiler_params=pltpu.CompilerParams(
            dimension_semantics=("arbitrary", "arbitrary"), vmem_limit_bytes=VMEM_LIMIT),
        name="peer_experts",
    )(h1, u_b, vt_b, cnt1, e1, rho2, e2, ln_g, ln_b)


def _pick_tile(n, target):
    t = min(n, target)
    while n % t:
        t //= 2
    return t


def kernel(x, meta, ln_in_g, ln_in_b, w_in, conv_w, conv_b, w_rg, b_rg, w_ig, b_ig, lru_L, w_proj_a, pool_w, pool_scale, w_proj_b, w_out, ln1_g, ln1_b, w_q, sub_keys, expert_u, expert_v, ln2_g, ln2_b):
    B, S, D = x.shape
    assert D == D_MODEL and w_in.shape[0] == DEPTH
    row = lambda v: v.reshape(1, -1).astype(_F32)
    l = 0
    mixer_weights = (
        row(ln_in_g), row(ln_in_b), w_in[l].astype(_BF16), conv_w[l].astype(_F32), row(conv_b[l]),
        jnp.concatenate([w_rg[l], w_ig[l]], axis=-1).astype(_BF16), row(b_rg[l]), row(b_ig[l]),
        row(lru_L[l]), w_proj_a[l].astype(_BF16), pool_w[l].astype(_BF16), row(pool_scale[l]),
        w_proj_b[l].astype(_BF16), w_out[l].astype(_BF16), row(ln1_g[l]), row(ln1_b[l]),
    )
    zero_state = (jnp.zeros((CONV_TAIL, D_RNN), _F32), jnp.zeros((POOL_TAIL, D_POOL), _F32),
                  jnp.zeros((SUBLANES, D_RNN), _F32))
    _, xr_t, xp_t, hs = _mixer(meta[None].astype(_F32), mixer_weights, zero_state,
                               tt=N_META, pos0=0, emit_state=True)
    (h1,) = _mixer(x, mixer_weights, (xr_t, xp_t, hs), tt=_pick_tile(S, 256), pos0=N_META,
                   emit_state=False)
    h1 = h1.reshape(B * S, D)
    N = B * S

    wq_t = w_q[l].reshape(D, PEER_HEADS, 2, PEER_HALF).transpose(2, 1, 3, 0)
    wq_t = wq_t.reshape(2 * PEER_HEADS * PEER_HALF, D).astype(_BF16)
    eye = jnp.eye(PEER_HEADS, dtype=_F32)
    keys_kh = jnp.einsum("hpkd,hg->pkhgd", sub_keys[l], eye)
    keys_kh = keys_kh.reshape(2, PEER_NKEYS * PEER_HEADS, PEER_HEADS * PEER_HALF).astype(_BF16)
    tn = _pick_tile(N, 512)
    cnt1, e1, rho2, e2 = _retrieve(h1, wq_t, keys_kh, tn=tn)

    u_b = expert_u[l].astype(_BF16)
    vt_b = expert_v[l].astype(_BF16).T
    out = _experts(h1, u_b, vt_b, cnt1, e1, rho2, e2, row(ln2_g[l]), row(ln2_b[l]), tn=tn, te=1024)
    return out.reshape(B, S, D).astype(x.dtype)
```

```python
import functools

import jax
import jax.numpy as jnp
from jax import lax
from jax.experimental import pallas as pl
from jax.experimental.pallas import tpu as pltpu

D_MODEL = 1024
N_META = 16
D_RNN = D_MODEL
N_RNN_BLOCKS = 4
RNN_BLOCK = D_RNN // N_RNN_BLOCKS
CONV_WIDTH = 4
LRU_C = 8.0
POOL_WINDOWS = (2, 4, 8, 16)
D_POOL = D_MODEL // 2
POOL_GROUP = D_POOL // len(POOL_WINDOWS)
PEER_HEADS = 8
PEER_NKEYS = 128
N_EXPERTS = PEER_NKEYS * PEER_NKEYS
PEER_HALF = 128
PEER_TOPK = 16
DEPTH = 1
ALPHA = (2.0 * DEPTH) ** 0.25
LN_EPS = 1e-5

SUBLANES = 8
PACK = 16
MM_PIECES = 4
LANES = 128
CONV_TAIL = SUBLANES
POOL_TAIL = 16
VMEM_LIMIT = 56 * 1024 * 1024

_F32 = jnp.float32
_BF16 = jnp.bfloat16


def _layer_norm(v, g, b):
    mu = jnp.mean(v, axis=-1, keepdims=True)
    vc = v - mu
    var = jnp.mean(vc * vc, axis=-1, keepdims=True)
    return vc * lax.rsqrt(var + LN_EPS) * g + b


def _gelu_tanh(v):
    return 0.5 * v * (1.0 + jnp.tanh(0.7978845608028654 * (v + 0.044715 * (v * v * v))))


def _gelu_sig(v):
    u2 = v * (1.5957691216057308 + 0.07135481627260025 * (v * v))
    return v / (1.0 + jnp.exp(-u2))


def _sigmoid(v):
    return 1.0 / (1.0 + jnp.exp(-v))


def _dot(a, b):
    return jnp.dot(a, b, preferred_element_type=_F32)


def _lru_scan(a, b, h_prev):
    tt, c = a.shape
    groups = tt // SUBLANES
    a3 = a.reshape(groups, SUBLANES, c)
    b3 = b.reshape(groups, SUBLANES, c)
    row = lax.broadcasted_iota(jnp.int32, a3.shape, 1)
    shift = 1
    while shift < SUBLANES:
        a_sh = pltpu.roll(a3, shift, axis=1)
        b_sh = pltpu.roll(b3, shift, axis=1)
        m = row >= shift
        b3 = jnp.where(m, a3 * b_sh + b3, b3)
        a3 = jnp.where(m, a3 * a_sh, a3)
        shift *= 2
    outs = []
    carry = h_prev
    for g in range(groups):
        hg = a3[g] * carry + b3[g]
        outs.append(hg)
        carry = hg[SUBLANES - 1:SUBLANES, :]
    return jnp.concatenate(outs, axis=0)


def _mixer_kernel(x_ref, lng_ref, lnb_ref, win_ref, convw_ref, convb_ref, wgate_ref, brg_ref,
                  big_ref, lrul_ref, wpa_ref, poolw_ref, pscale_ref, wpb_ref, wout_ref,
                  ln1g_ref, ln1b_ref, xr0_ref, xp0_ref, hs0_ref,
                  h1_ref, *rest, tt, pos0, emit_state):
    if emit_state:
        xr_out, xp_out, hs_out, xr_buf, xp_buf, hs_buf, rg_buf, mx_buf = rest
    else:
        xr_buf, xp_buf, hs_buf, rg_buf, mx_buf = rest
    t = pl.program_id(1)

    @pl.when(t == 0)
    def _():
        xr_buf[0:CONV_TAIL, :] = xr0_ref[...]
        xp_buf[0:POOL_TAIL, :] = xp0_ref[...]
        hs_buf[...] = hs0_ref[...]

    x = x_ref[0]
    h0 = _layer_norm(x, lng_ref[...], lnb_ref[...])
    h0b = h0.astype(_BF16)

    xr_buf[CONV_TAIL:CONV_TAIL + tt, :] = _dot(h0b, win_ref[:, 0:D_RNN])
    lsig = lrul_ref[...]
    log_sig = jnp.minimum(lsig, 0.0) - jnp.log1p(jnp.exp(-jnp.abs(lsig)))
    for n in range(N_RNN_BLOCKS):
        cs = slice(n * RNN_BLOCK, (n + 1) * RNN_BLOCK)
        xc = convb_ref[:, cs]
        for j in range(CONV_WIDTH):
            off = CONV_TAIL - (CONV_WIDTH - 1) + j
            xc = xc + convw_ref[j:j + 1, cs] * xr_buf[off:off + tt, cs]
        pre = _dot(xc.astype(_BF16), wgate_ref[n])
        r = _sigmoid(pre[:, :RNN_BLOCK] + brg_ref[:, cs])
        i = _sigmoid(pre[:, RNN_BLOCK:] + big_ref[:, cs])
        log_a = (LRU_C * log_sig[:, cs]) * r
        a = jnp.exp(log_a)
        mult = jnp.sqrt(1.0 - a * a)
        h = _lru_scan(a, mult * i * xc, hs_buf[0:1, cs])
        hs_buf[:, cs] = jnp.broadcast_to(h[tt - 1:tt, :], (SUBLANES, RNN_BLOCK))
        gate = _dot(h0b, win_ref[:, D_RNN + n * RNN_BLOCK:D_RNN + (n + 1) * RNN_BLOCK])
        rg_buf[:, cs] = (h * _gelu_tanh(gate)).astype(_BF16)
    y_a = _dot(rg_buf[...], wpa_ref[...])

    xp_buf[POOL_TAIL:POOL_TAIL + tt, :] = _dot(h0b, win_ref[:, 2 * D_RNN:2 * D_RNN + D_POOL])
    if pos0 < max(POOL_WINDOWS):
        pos = (lax.broadcasted_iota(jnp.int32, (tt, 1), 0) + (pos0 + 1) + t * tt).astype(_F32)
    for g, w in enumerate(POOL_WINDOWS):
        cs = slice(g * POOL_GROUP, (g + 1) * POOL_GROUP)
        cur = xp_buf[POOL_TAIL:POOL_TAIL + tt, cs]
        acc = cur
        for j in range(1, w):
            acc = acc + xp_buf[POOL_TAIL - j:POOL_TAIL - j + tt, cs]
        if pos0 < max(POOL_WINDOWS):
            mean = acc / jnp.minimum(pos, float(w))
        else:
            mean = acc * (1.0 / w)
        mixed = _dot((mean - cur).astype(_BF16), poolw_ref[g]) * pscale_ref[:, cs]
        mx_buf[:, cs] = mixed.astype(_BF16)
    y_b = _dot(mx_buf[...], wpb_ref[...])

    g_off = 2 * D_RNN + D_POOL
    gate_a = _sigmoid(_dot(h0b, win_ref[:, g_off:g_off + D_MODEL]))
    gate_b = _sigmoid(_dot(h0b, win_ref[:, g_off + D_MODEL:g_off + 2 * D_MODEL]))
    merged = gate_a * y_a + gate_b * y_b
    mix = _dot(merged.astype(_BF16), wout_ref[...])
    h1_ref[0] = _layer_norm(ALPHA * h0 + mix, ln1g_ref[...], ln1b_ref[...])

    xr_buf[0:CONV_TAIL, :] = xr_buf[tt:tt + CONV_TAIL, :]
    xp_buf[0:POOL_TAIL, :] = xp_buf[tt:tt + POOL_TAIL, :]
    if emit_state:
        xr_out[...] = xr_buf[0:CONV_TAIL, :]
        xp_out[...] = xp_buf[0:POOL_TAIL, :]
        hs_out[...] = hs_buf[...]


def _const_spec(shape):
    nd = len(shape)
    return pl.BlockSpec(shape, lambda b, t: (0,) * nd, pipeline_mode=pl.Buffered(1))


def _mixer(x, weights, state, *, tt, pos0, emit_state):
    B, S, D = x.shape
    assert S % tt == 0 and tt % POOL_TAIL == 0
    in_specs = [pl.BlockSpec((1, tt, D), lambda b, t: (b, t, 0))]
    in_specs += [_const_spec(w.shape) for w in weights]
    in_specs += [_const_spec(s.shape) for s in state]
    out_shape = [jax.ShapeDtypeStruct((B, S, D), _F32)]
    out_specs = [pl.BlockSpec((1, tt, D), lambda b, t: (b, t, 0))]
    if emit_state:
        assert B == 1 and S == tt
        out_shape += [jax.ShapeDtypeStruct(s.shape, _F32) for s in state]
        out_specs += [pl.BlockSpec(s.shape, lambda b, t: (0, 0)) for s in state]
    scratch = [
        pltpu.VMEM((CONV_TAIL + tt, D_RNN), _F32),
        pltpu.VMEM((POOL_TAIL + tt, D_POOL), _F32),
        pltpu.VMEM((SUBLANES, D_RNN), _F32),
        pltpu.VMEM((tt, D_RNN), _BF16),
        pltpu.VMEM((tt, D_POOL), _BF16),
    ]
    return pl.pallas_call(
        functools.partial(_mixer_kernel, tt=tt, pos0=pos0, emit_state=emit_state),
        grid=(B, S // tt),
        in_specs=in_specs,
        out_specs=out_specs,
        out_shape=out_shape,
        scratch_shapes=scratch,
        compiler_params=pltpu.CompilerParams(
            dimension_semantics=("arbitrary", "arbitrary"), vmem_limit_bytes=VMEM_LIMIT),
        name="mixer_meta" if emit_state else "mixer",
    )(x, *weights, *state)


def _oddeven_merge_sort_pairs(n):
    pairs = []

    def merge(lo, hi, r):
        step = r * 2
        if step < hi - lo:
            merge(lo, hi, step)
            merge(lo + r, hi, step)
            for i in range(lo + r, hi - r, step):
                pairs.append((i, i + r))
        else:
            pairs.append((lo, lo + r))

    def sort(lo, hi):
        if hi - lo >= 1:
            mid = lo + (hi - lo) // 2
            sort(lo, mid)
            sort(mid + 1, hi)
            merge(lo, hi, 1)

    sort(0, n - 1)
    return pairs


_SORT16 = _oddeven_merge_sort_pairs(PEER_TOPK)


def _cmpx(v, i, j):
    hi = jnp.maximum(v[i], v[j])
    lo = jnp.minimum(v[i], v[j])
    v[i] = hi
    v[j] = lo


def _sort16_desc(v):
    v = list(v)
    for i, j in _SORT16:
        _cmpx(v, i, j)
    return v


def _merge_top16(a, b):
    c = list(a)
    for k in range(len(b)):
        c[PEER_TOPK - 1 - k] = jnp.maximum(a[PEER_TOPK - 1 - k], b[k])
    stride = PEER_TOPK // 2
    while stride >= 1:
        for i in range(PEER_TOPK):
            if i & stride == 0:
                _cmpx(c, i, i + stride)
        stride //= 2
    return c


def _top16_desc(vals):
    runs = [_sort16_desc(vals[i:i + PEER_TOPK]) for i in range(0, len(vals), PEER_TOPK)]
    while len(runs) > 1:
        runs = [_merge_top16(runs[i], runs[i + 1]) for i in range(0, len(runs), 2)]
    return runs[0]


_CAND = [(i, j) for i in range(PEER_TOPK) for j in range(PEER_TOPK) if (i + 1) * (j + 1) <= PEER_TOPK]


def _dup_bf16_words(v):
    bits = lax.bitcast_convert_type(v.astype(_BF16).astype(_F32), jnp.uint32)
    return bits | (bits >> 16)


def _retrieve_kernel(ht_ref, wq_ref, keys_ref, cnt1_ref, e1_ref, rho2_ref, e2_ref,
                     s_buf, x_buf, *, tn, chunk):
    q_t = _dot(wq_ref[...], ht_ref[...]).astype(_BF16)
    half_rows = PEER_HEADS * PEER_HALF
    for p in range(2):
        s_buf[p] = _dot(keys_ref[p], q_t[p * half_rows:(p + 1) * half_rows, :])

    neg = jnp.full((PEER_HEADS, chunk), -jnp.inf, _F32)

    def slab(k):
        return pl.ds(pl.multiple_of(k * PEER_HEADS, PEER_HEADS), PEER_HEADS)

    def chunk_body(c, _):
        ls = pl.ds(pl.multiple_of(c * chunk, chunk), chunk)
        a = _top16_desc([s_buf[0, k * PEER_HEADS:(k + 1) * PEER_HEADS, ls] for k in range(PEER_NKEYS)])
        b = _top16_desc([s_buf[1, k * PEER_HEADS:(k + 1) * PEER_HEADS, ls] for k in range(PEER_NKEYS)])
        rows = {}
        for i, j in _CAND:
            rows.setdefault(i, []).append(a[i] + b[j])
        top = rows[0]
        rest = [v for i in range(1, PEER_TOPK) for v in rows[i]]
        while rest:
            grp, rest = rest[:PEER_TOPK], rest[PEER_TOPK:]
            grp = grp + [neg] * (PEER_TOPK - len(grp))
            top = _merge_top16(top, _sort16_desc(grp))
        c16 = top[PEER_TOPK - 1]
        ea = [jnp.exp(a[i] - a[0]) for i in range(PEER_TOPK)]
        eb = [jnp.exp(b[j] - b[0]) for j in range(PEER_TOPK)]
        z = jnp.zeros((PEER_HEADS, chunk), _F32)
        for i, j in _CAND:
            z = z + jnp.where(a[i] + b[j] >= c16, ea[i] * eb[j], 0.0)
        zinv = 1.0 / z

        def key_body(k, _):
            s1 = s_buf[0, slab(k), ls]
            s2 = s_buf[1, slab(k), ls]
            cnt = jnp.zeros((PEER_HEADS, chunk), _F32)
            rho = jnp.zeros((PEER_HEADS, chunk), _F32)
            for j in range(PEER_TOPK):
                cnt = cnt + jnp.where(s1 + b[j] >= c16, 1.0, 0.0)
                rho = rho + jnp.where(b[j] > s2, 1.0, 0.0)
            x_buf[0, slab(k), :] = lax.bitcast_convert_type(_dup_bf16_words(cnt), _F32)
            x_buf[1, slab(k), :] = lax.bitcast_convert_type(
                _dup_bf16_words(jnp.exp(s1 - a[0]) * zinv), _F32)
            x_buf[2, slab(k), :] = rho
            x_buf[3, slab(k), :] = jnp.exp(s2 - b[0])
            return 0

        lax.fori_loop(0, PEER_NKEYS, key_body, 0)

        for h in range(PEER_HEADS):
            rows_h = pl.ds(h, PEER_NKEYS, stride=PEER_HEADS)
            cnt1_ref[h, :, ls] = lax.bitcast_convert_type(x_buf[0, rows_h, :], jnp.uint32)
            e1_ref[h, :, ls] = lax.bitcast_convert_type(x_buf[1, rows_h, :], jnp.uint32)
            rho2_ref[h, :, ls] = x_buf[2, rows_h, :].astype(_BF16)
            e2_ref[h, :, ls] = x_buf[3, rows_h, :].astype(_BF16)
        return 0

    lax.fori_loop(0, tn // chunk, chunk_body, 0)


def _retrieve(h_t, wq_t, keys_kh, *, tn, chunk=LANES):
    N = h_t.shape[1]
    assert N % tn == 0 and tn % chunk == 0 and chunk == LANES
    sds = lambda dt: jax.ShapeDtypeStruct((PEER_HEADS, PEER_NKEYS, N), dt)
    out_spec = pl.BlockSpec((PEER_HEADS, PEER_NKEYS, tn), lambda i: (0, 0, i))
    return pl.pallas_call(
        functools.partial(_retrieve_kernel, tn=tn, chunk=chunk),
        grid=(N // tn,),
        in_specs=[
            pl.BlockSpec((D_MODEL, tn), lambda i: (0, i)),
            pl.BlockSpec(wq_t.shape, lambda i: (0, 0), pipeline_mode=pl.Buffered(1)),
            pl.BlockSpec(keys_kh.shape, lambda i: (0, 0, 0), pipeline_mode=pl.Buffered(1)),
        ],
        out_specs=[out_spec] * 4,
        out_shape=[sds(jnp.uint32), sds(jnp.uint32), sds(_BF16), sds(_BF16)],
        scratch_shapes=[
            pltpu.VMEM((2, PEER_NKEYS * PEER_HEADS, tn), _F32),
            pltpu.VMEM((4, PEER_NKEYS * PEER_HEADS, chunk), _F32),
        ],
        compiler_params=pltpu.CompilerParams(
            dimension_semantics=("arbitrary",), vmem_limit_bytes=VMEM_LIMIT),
        name="peer_retrieve",
    )(h_t, wq_t, keys_kh)


def _experts_kernel(ht_ref, h1_ref, u_ref, vt_ref, cnt1_ref, e1_ref, rho2_ref, e2_ref, g_ref, b_ref,
                    out_ref, z0_buf, z1_buf, gz0_buf, gz1_buf, acc_buf, *, tn, te, n_e, lane_chunk):
    j = pl.program_id(0)
    rows = te // PEER_NKEYS
    e_prev = (j + n_e - 1) % n_e
    e_pp = (j + n_e - 2) % n_e

    @pl.when(j == 0)
    def _():
        z1_buf[...] = jnp.zeros_like(z1_buf)
        gz0_buf[...] = jnp.zeros_like(gz0_buf)
        gz1_buf[...] = jnp.zeros_like(gz1_buf)

    @pl.when(jnp.logical_or(e_pp == 0, j == 0))
    def _():
        acc_buf[...] = jnp.zeros_like(acc_buf)

    def stages(z_new, z_cur, gz_cur, gz_old):
        k1_base = pl.multiple_of(e_prev * rows, rows)
        zero = jnp.zeros((PACK, lane_chunk), _BF16)

        def weights(r_lo, r_hi):
            for c in range(tn // lane_chunk):
                ls = slice(c * lane_chunk, (c + 1) * lane_chunk)
                cnt1 = [cnt1_ref[h, pl.ds(k1_base, rows), ls] for h in range(PEER_HEADS)]
                e1 = [e1_ref[h, pl.ds(k1_base, rows), ls] for h in range(PEER_HEADS)]
                for r in range(r_lo, r_hi):
                    def bcast(w):
                        w8 = jnp.broadcast_to(w[r:r + 1, :], (SUBLANES, lane_chunk))
                        return pltpu.bitcast(w8, _BF16)
                    n_q = PEER_NKEYS // PACK
                    g = [None] * n_q
                    for h in range(PEER_HEADS):
                        cb = bcast(cnt1[h])
                        eb = bcast(e1[h])
                        for q in range(n_q):
                            ks = slice(q * PACK, (q + 1) * PACK)
                            m = rho2_ref[h, ks, ls] < cb
                            term = jnp.where(m, e2_ref[h, ks, ls], zero) * eb
                            g[q] = term if g[q] is None else g[q] + term
                    for q in range(n_q):
                        rs = slice(r * PEER_NKEYS + q * PACK, r * PEER_NKEYS + (q + 1) * PACK)
                        zz = z_cur[rs, ls].astype(_BF16)
                        gz_cur[rs, ls] = g[q] * _gelu_sig(zz)

        n_piece = 2 * MM_PIECES
        for p in range(n_piece):
            if p < MM_PIECES:
                ms = slice(p * te // MM_PIECES, (p + 1) * te // MM_PIECES)
                z_new[ms, :] = _dot(u_ref[ms, :], ht_ref[...])
            else:
                q = p - MM_PIECES
                ms = slice(q * D_MODEL // MM_PIECES, (q + 1) * D_MODEL // MM_PIECES)
                acc_buf[ms, :] += _dot(vt_ref[ms, :], gz_old[...])
            weights(p * rows // n_piece, (p + 1) * rows // n_piece)

    @pl.when(j % 2 == 0)
    def _():
        stages(z0_buf, z1_buf, gz1_buf, gz0_buf)

    @pl.when(j % 2 == 1)
    def _():
        stages(z1_buf, z0_buf, gz0_buf, gz1_buf)

    @pl.when(jnp.logical_and(e_pp == n_e - 1, j >= 2))
    def _():
        ffn = acc_buf[...].T
        out_ref[...] = _layer_norm(ALPHA * h1_ref[...] + ffn, g_ref[...], b_ref[...])


def _experts(h_t, h1, u_b, vt_b, cnt1, e1, rho2, e2, ln_g, ln_b, *, tn, te, lane_chunk=256):
    N = h1.shape[0]
    assert N % tn == 0 and N_EXPERTS % te == 0 and te % (SUBLANES * PEER_NKEYS) == 0
    n_e = N_EXPERTS // te
    total = (N // tn) * n_e
    last = total - 1
    tile0 = lambda j: jnp.minimum(j, last)
    tile1 = lambda j: jnp.clip(j - 1, 0, last)
    tile2 = lambda j: jnp.clip(j - 2, 0, last)
    sel_spec = pl.BlockSpec((PEER_HEADS, PEER_NKEYS, tn), lambda j: (0, 0, tile1(j) // n_e))
    return pl.pallas_call(
        functools.partial(_experts_kernel, tn=tn, te=te, n_e=n_e, lane_chunk=min(lane_chunk, tn)),
        grid=(total + 2,),
        in_specs=[
            pl.BlockSpec((D_MODEL, tn), lambda j: (0, tile0(j) // n_e)),
            pl.BlockSpec((tn, D_MODEL), lambda j: (tile2(j) // n_e, 0)),
            pl.BlockSpec((te, D_MODEL), lambda j: (tile0(j) % n_e, 0)),
            pl.BlockSpec((D_MODEL, te), lambda j: (0, tile2(j) % n_e)),
            sel_spec, sel_spec, sel_spec, sel_spec,
            pl.BlockSpec((1, D_MODEL), lambda j: (0, 0)),
            pl.BlockSpec((1, D_MODEL), lambda j: (0, 0)),
        ],
        out_specs=pl.BlockSpec((tn, D_MODEL), lambda j: (tile2(j) // n_e, 0)),
        out_shape=jax.ShapeDtypeStruct((N, D_MODEL), _F32),
        scratch_shapes=[
            pltpu.VMEM((te, tn), _F32),
            pltpu.VMEM((te, tn), _F32),
            pltpu.VMEM((te, tn), _BF16),
            pltpu.VMEM((te, tn), _BF16),
            pltpu.VMEM((D_MODEL, tn), _F32),
        ],
        compiler_params=pltpu.CompilerParams(
            dimension_semantics=("arbitrary",), vmem_limit_bytes=VMEM_LIMIT),
        name="peer_experts",
    )(h_t, h1, u_b, vt_b, cnt1, e1, rho2, e2, ln_g, ln_b)


def _pick_tile(n, target):
    t = min(n, target)
    while n % t:
        t //= 2
    return t


def kernel(x, meta, ln_in_g, ln_in_b, w_in, conv_w, conv_b, w_rg, b_rg, w_ig, b_ig, lru_L, w_proj_a, pool_w, pool_scale, w_proj_b, w_out, ln1_g, ln1_b, w_q, sub_keys, expert_u, expert_v, ln2_g, ln2_b):
    B, S, D = x.shape
    assert D == D_MODEL and w_in.shape[0] == DEPTH
    row = lambda v: v.reshape(1, -1).astype(_F32)
    l = 0
    mixer_weights = (
        row(ln_in_g), row(ln_in_b), w_in[l].astype(_BF16), conv_w[l].astype(_F32), row(conv_b[l]),
        jnp.concatenate([w_rg[l], w_ig[l]], axis=-1).astype(_BF16), row(b_rg[l]), row(b_ig[l]),
        row(lru_L[l]), w_proj_a[l].astype(_BF16), pool_w[l].astype(_BF16), row(pool_scale[l]),
        w_proj_b[l].astype(_BF16), w_out[l].astype(_BF16), row(ln1_g[l]), row(ln1_b[l]),
    )
    zero_state = (jnp.zeros((CONV_TAIL, D_RNN), _F32), jnp.zeros((POOL_TAIL, D_POOL), _F32),
                  jnp.zeros((SUBLANES, D_RNN), _F32))
    _, xr_t, xp_t, hs = _mixer(meta[None].astype(_F32), mixer_weights, zero_state,
                               tt=N_META, pos0=0, emit_state=True)
    (h1,) = _mixer(x, mixer_weights, (xr_t, xp_t, hs), tt=_pick_tile(S, 256), pos0=N_META,
                   emit_state=False)
    h1 = h1.reshape(B * S, D)
    N = B * S

    wq_t = w_q[l].reshape(D, PEER_HEADS, 2, PEER_HALF).transpose(2, 1, 3, 0)
    wq_t = wq_t.reshape(2 * PEER_HEADS * PEER_HALF, D).astype(_BF16)
    eye = jnp.eye(PEER_HEADS, dtype=_F32)
    keys_kh = jnp.einsum("hpkd,hg->pkhgd", sub_keys[l], eye)
    keys_kh = keys_kh.reshape(2, PEER_NKEYS * PEER_HEADS, PEER_HEADS * PEER_HALF).astype(_BF16)
    tn = _pick_tile(N, 512)
    h_t = h1.astype(_BF16).T
    cnt1, e1, rho2, e2 = _retrieve(h_t, wq_t, keys_kh, tn=tn)

    u_b = expert_u[l].astype(_BF16)
    vt_b = expert_v[l].astype(_BF16).T
    out = _experts(h_t, h1, u_b, vt_b, cnt1, e1, rho2, e2, row(ln2_g[l]), row(ln2_b[l]),
                   tn=tn, te=1024)
    return out.reshape(B, S, D).astype(x.dtype)
```

```python
import functools

import jax
import jax.numpy as jnp
from jax import lax
from jax.experimental import pallas as pl
from jax.experimental.pallas import tpu as pltpu

D_MODEL = 1024
N_META = 16
D_RNN = D_MODEL
N_RNN_BLOCKS = 4
RNN_BLOCK = D_RNN // N_RNN_BLOCKS
CONV_WIDTH = 4
LRU_C = 8.0
POOL_WINDOWS = (2, 4, 8, 16)
D_POOL = D_MODEL // 2
POOL_GROUP = D_POOL // len(POOL_WINDOWS)
PEER_HEADS = 8
PEER_NKEYS = 128
N_EXPERTS = PEER_NKEYS * PEER_NKEYS
PEER_HALF = 128
PEER_TOPK = 16
DEPTH = 1
ALPHA = (2.0 * DEPTH) ** 0.25
LN_EPS = 1e-5

SUBLANES = 8
PACK = 16
MM_PIECES = 4
KEY_UNROLL = 4
LANES = 128
CONV_TAIL = SUBLANES
POOL_TAIL = 16
VMEM_LIMIT = 56 * 1024 * 1024

_F32 = jnp.float32
_BF16 = jnp.bfloat16


def _layer_norm(v, g, b):
    mu = jnp.mean(v, axis=-1, keepdims=True)
    vc = v - mu
    var = jnp.mean(vc * vc, axis=-1, keepdims=True)
    return vc * lax.rsqrt(var + LN_EPS) * g + b


def _gelu_tanh(v):
    return 0.5 * v * (1.0 + jnp.tanh(0.7978845608028654 * (v + 0.044715 * (v * v * v))))


def _gelu_sig(v):
    u2 = v * (1.5957691216057308 + 0.07135481627260025 * (v * v))
    return v / (1.0 + jnp.exp(-u2))


def _sigmoid(v):
    return 1.0 / (1.0 + jnp.exp(-v))


def _dot(a, b):
    return jnp.dot(a, b, preferred_element_type=_F32)


def _lru_scan(a, b, h_prev):
    tt, c = a.shape
    groups = tt // SUBLANES
    a3 = a.reshape(groups, SUBLANES, c)
    b3 = b.reshape(groups, SUBLANES, c)
    row = lax.broadcasted_iota(jnp.int32, a3.shape, 1)
    shift = 1
    while shift < SUBLANES:
        a_sh = pltpu.roll(a3, shift, axis=1)
        b_sh = pltpu.roll(b3, shift, axis=1)
        m = row >= shift
        b3 = jnp.where(m, a3 * b_sh + b3, b3)
        a3 = jnp.where(m, a3 * a_sh, a3)
        shift *= 2
    outs = []
    carry = h_prev
    for g in range(groups):
        hg = a3[g] * carry + b3[g]
        outs.append(hg)
        carry = hg[SUBLANES - 1:SUBLANES, :]
    return jnp.concatenate(outs, axis=0)


def _mixer_kernel(x_ref, lng_ref, lnb_ref, win_ref, convw_ref, convb_ref, wgate_ref, brg_ref,
                  big_ref, lrul_ref, wpa_ref, poolw_ref, pscale_ref, wpb_ref, wout_ref,
                  ln1g_ref, ln1b_ref, xr0_ref, xp0_ref, hs0_ref,
                  h1_ref, *rest, tt, pos0, emit_state):
    if emit_state:
        xr_out, xp_out, hs_out, xr_buf, xp_buf, hs_buf, rg_buf, mx_buf = rest
    else:
        xr_buf, xp_buf, hs_buf, rg_buf, mx_buf = rest
    t = pl.program_id(1)

    @pl.when(t == 0)
    def _():
        xr_buf[0:CONV_TAIL, :] = xr0_ref[...]
        xp_buf[0:POOL_TAIL, :] = xp0_ref[...]
        hs_buf[...] = hs0_ref[...]

    x = x_ref[0]
    h0 = _layer_norm(x, lng_ref[...], lnb_ref[...])
    h0b = h0.astype(_BF16)

    xr_buf[CONV_TAIL:CONV_TAIL + tt, :] = _dot(h0b, win_ref[:, 0:D_RNN])
    lsig = lrul_ref[...]
    log_sig = jnp.minimum(lsig, 0.0) - jnp.log1p(jnp.exp(-jnp.abs(lsig)))
    for n in range(N_RNN_BLOCKS):
        cs = slice(n * RNN_BLOCK, (n + 1) * RNN_BLOCK)
        xc = convb_ref[:, cs]
        for j in range(CONV_WIDTH):
            off = CONV_TAIL - (CONV_WIDTH - 1) + j
            xc = xc + convw_ref[j:j + 1, cs] * xr_buf[off:off + tt, cs]
        pre = _dot(xc.astype(_BF16), wgate_ref[n])
        r = _sigmoid(pre[:, :RNN_BLOCK] + brg_ref[:, cs])
        i = _sigmoid(pre[:, RNN_BLOCK:] + big_ref[:, cs])
        log_a = (LRU_C * log_sig[:, cs]) * r
        a = jnp.exp(log_a)
        mult = jnp.sqrt(1.0 - a * a)
        h = _lru_scan(a, mult * i * xc, hs_buf[0:1, cs])
        hs_buf[:, cs] = jnp.broadcast_to(h[tt - 1:tt, :], (SUBLANES, RNN_BLOCK))
        gate = _dot(h0b, win_ref[:, D_RNN + n * RNN_BLOCK:D_RNN + (n + 1) * RNN_BLOCK])
        rg_buf[:, cs] = (h * _gelu_tanh(gate)).astype(_BF16)
    y_a = _dot(rg_buf[...], wpa_ref[...])

    xp_buf[POOL_TAIL:POOL_TAIL + tt, :] = _dot(h0b, win_ref[:, 2 * D_RNN:2 * D_RNN + D_POOL])
    if pos0 < max(POOL_WINDOWS):
        pos = (lax.broadcasted_iota(jnp.int32, (tt, 1), 0) + (pos0 + 1) + t * tt).astype(_F32)
    for g, w in enumerate(POOL_WINDOWS):
        cs = slice(g * POOL_GROUP, (g + 1) * POOL_GROUP)
        cur = xp_buf[POOL_TAIL:POOL_TAIL + tt, cs]
        acc = cur
        for j in range(1, w):
            acc = acc + xp_buf[POOL_TAIL - j:POOL_TAIL - j + tt, cs]
        if pos0 < max(POOL_WINDOWS):
            mean = acc / jnp.minimum(pos, float(w))
        else:
            mean = acc * (1.0 / w)
        mixed = _dot((mean - cur).astype(_BF16), poolw_ref[g]) * pscale_ref[:, cs]
        mx_buf[:, cs] = mixed.astype(_BF16)
    y_b = _dot(mx_buf[...], wpb_ref[...])

    g_off = 2 * D_RNN + D_POOL
    gate_a = _sigmoid(_dot(h0b, win_ref[:, g_off:g_off + D_MODEL]))
    gate_b = _sigmoid(_dot(h0b, win_ref[:, g_off + D_MODEL:g_off + 2 * D_MODEL]))
    merged = gate_a * y_a + gate_b * y_b
    mix = _dot(merged.astype(_BF16), wout_ref[...])
    h1_ref[0] = _layer_norm(ALPHA * h0 + mix, ln1g_ref[...], ln1b_ref[...])

    xr_buf[0:CONV_TAIL, :] = xr_buf[tt:tt + CONV_TAIL, :]
    xp_buf[0:POOL_TAIL, :] = xp_buf[tt:tt + POOL_TAIL, :]
    if emit_state:
        xr_out[...] = xr_buf[0:CONV_TAIL, :]
        xp_out[...] = xp_buf[0:POOL_TAIL, :]
        hs_out[...] = hs_buf[...]


def _const_spec(shape):
    nd = len(shape)
    return pl.BlockSpec(shape, lambda b, t: (0,) * nd, pipeline_mode=pl.Buffered(1))


def _mixer(x, weights, state, *, tt, pos0, emit_state):
    B, S, D = x.shape
    assert S % tt == 0 and tt % POOL_TAIL == 0
    in_specs = [pl.BlockSpec((1, tt, D), lambda b, t: (b, t, 0))]
    in_specs += [_const_spec(w.shape) for w in weights]
    in_specs += [_const_spec(s.shape) for s in state]
    out_shape = [jax.ShapeDtypeStruct((B, S, D), _F32)]
    out_specs = [pl.BlockSpec((1, tt, D), lambda b, t: (b, t, 0))]
    if emit_state:
        assert B == 1 and S == tt
        out_shape += [jax.ShapeDtypeStruct(s.shape, _F32) for s in state]
        out_specs += [pl.BlockSpec(s.shape, lambda b, t: (0, 0)) for s in state]
    scratch = [
        pltpu.VMEM((CONV_TAIL + tt, D_RNN), _F32),
        pltpu.VMEM((POOL_TAIL + tt, D_POOL), _F32),
        pltpu.VMEM((SUBLANES, D_RNN), _F32),
        pltpu.VMEM((tt, D_RNN), _BF16),
        pltpu.VMEM((tt, D_POOL), _BF16),
    ]
    return pl.pallas_call(
        functools.partial(_mixer_kernel, tt=tt, pos0=pos0, emit_state=emit_state),
        grid=(B, S // tt),
        in_specs=in_specs,
        out_specs=out_specs,
        out_shape=out_shape,
        scratch_shapes=scratch,
        compiler_params=pltpu.CompilerParams(
            dimension_semantics=("arbitrary", "arbitrary"), vmem_limit_bytes=VMEM_LIMIT),
        name="mixer_meta" if emit_state else "mixer",
    )(x, *weights, *state)


def _oddeven_merge_sort_pairs(n):
    pairs = []

    def merge(lo, hi, r):
        step = r * 2
        if step < hi - lo:
            merge(lo, hi, step)
            merge(lo + r, hi, step)
            for i in range(lo + r, hi - r, step):
                pairs.append((i, i + r))
        else:
            pairs.append((lo, lo + r))

    def sort(lo, hi):
        if hi - lo >= 1:
            mid = lo + (hi - lo) // 2
            sort(lo, mid)
            sort(mid + 1, hi)
            merge(lo, hi, 1)

    sort(0, n - 1)
    return pairs


_SORT16 = _oddeven_merge_sort_pairs(PEER_TOPK)


def _cmpx(v, i, j):
    hi = jnp.maximum(v[i], v[j])
    lo = jnp.minimum(v[i], v[j])
    v[i] = hi
    v[j] = lo


def _sort16_desc(v):
    v = list(v)
    for i, j in _SORT16:
        _cmpx(v, i, j)
    return v


def _merge_top16(a, b):
    c = list(a)
    for k in range(len(b)):
        c[PEER_TOPK - 1 - k] = jnp.maximum(a[PEER_TOPK - 1 - k], b[k])
    stride = PEER_TOPK // 2
    while stride >= 1:
        for i in range(PEER_TOPK):
            if i & stride == 0:
                _cmpx(c, i, i + stride)
        stride //= 2
    return c


def _top16_desc(vals):
    runs = [_sort16_desc(vals[i:i + PEER_TOPK]) for i in range(0, len(vals), PEER_TOPK)]
    while len(runs) > 1:
        runs = [_merge_top16(runs[i], runs[i + 1]) for i in range(0, len(runs), 2)]
    return runs[0]


_CAND = [(i, j) for i in range(PEER_TOPK) for j in range(PEER_TOPK) if (i + 1) * (j + 1) <= PEER_TOPK]


def _dup_bf16_words(v):
    bits = lax.bitcast_convert_type(v.astype(_BF16).astype(_F32), jnp.uint32)
    return bits | (bits >> 16)


def _retrieve_kernel(ht_ref, wq_ref, keys_ref, cnt1_ref, e1_ref, rho2_ref, e2_ref,
                     s_buf, x_buf, *, tn, chunk):
    q_t = _dot(wq_ref[...], ht_ref[...]).astype(_BF16)
    half_rows = PEER_HEADS * PEER_HALF
    for p in range(2):
        s_buf[p] = _dot(keys_ref[p], q_t[p * half_rows:(p + 1) * half_rows, :])

    neg = jnp.full((PEER_HEADS, chunk), -jnp.inf, _F32)

    def slab(k):
        return pl.ds(pl.multiple_of(k * PEER_HEADS, PEER_HEADS), PEER_HEADS)

    def chunk_body(c, _):
        ls = pl.ds(pl.multiple_of(c * chunk, chunk), chunk)
        a = _top16_desc([s_buf[0, k * PEER_HEADS:(k + 1) * PEER_HEADS, ls] for k in range(PEER_NKEYS)])
        b = _top16_desc([s_buf[1, k * PEER_HEADS:(k + 1) * PEER_HEADS, ls] for k in range(PEER_NKEYS)])
        rows = {}
        for i, j in _CAND:
            rows.setdefault(i, []).append(a[i] + b[j])
        top = rows[0]
        rest = [v for i in range(1, PEER_TOPK) for v in rows[i]]
        while rest:
            grp, rest = rest[:PEER_TOPK], rest[PEER_TOPK:]
            grp = grp + [neg] * (PEER_TOPK - len(grp))
            top = _merge_top16(top, _sort16_desc(grp))
        c16 = top[PEER_TOPK - 1]
        ea = [jnp.exp(a[i] - a[0]) for i in range(PEER_TOPK)]
        eb = [jnp.exp(b[j] - b[0]) for j in range(PEER_TOPK)]
        z = jnp.zeros((PEER_HEADS, chunk), _F32)
        for i, j in _CAND:
            z = z + jnp.where(a[i] + b[j] >= c16, ea[i] * eb[j], 0.0)
        zinv = 1.0 / z

        def key_body(kk, _):
            for u in range(KEY_UNROLL):
                k = kk * KEY_UNROLL + u
                s1 = s_buf[0, slab(k), ls]
                s2 = s_buf[1, slab(k), ls]
                cnt = jnp.zeros((PEER_HEADS, chunk), _F32)
                rho = jnp.zeros((PEER_HEADS, chunk), _F32)
                for j in range(PEER_TOPK):
                    cnt = jnp.where(s1 + b[j] >= c16, j + 1.0, cnt)
                    rho = jnp.where(b[j] > s2, j + 1.0, rho)
                x_buf[0, slab(k), :] = lax.bitcast_convert_type(_dup_bf16_words(cnt), _F32)
                x_buf[1, slab(k), :] = lax.bitcast_convert_type(
                    _dup_bf16_words(jnp.exp(s1 - a[0]) * zinv), _F32)
                x_buf[2, slab(k), :] = rho
                x_buf[3, slab(k), :] = jnp.exp(s2 - b[0])
            return 0

        lax.fori_loop(0, PEER_NKEYS // KEY_UNROLL, key_body, 0)

        for h in range(PEER_HEADS):
            rows_h = pl.ds(h, PEER_NKEYS, stride=PEER_HEADS)
            cnt1_ref[h, :, ls] = lax.bitcast_convert_type(x_buf[0, rows_h, :], jnp.uint32)
            e1_ref[h, :, ls] = lax.bitcast_convert_type(x_buf[1, rows_h, :], jnp.uint32)
            rho2_ref[h, :, ls] = x_buf[2, rows_h, :].astype(_BF16)
            e2_ref[h, :, ls] = x_buf[3, rows_h, :].astype(_BF16)
        return 0

    lax.fori_loop(0, tn // chunk, chunk_body, 0)


def _retrieve(h_t, wq_t, keys_kh, *, tn, chunk=LANES):
    N = h_t.shape[1]
    assert N % tn == 0 and tn % chunk == 0 and chunk == LANES
    sds = lambda dt: jax.ShapeDtypeStruct((PEER_HEADS, PEER_NKEYS, N), dt)
    out_spec = pl.BlockSpec((PEER_HEADS, PEER_NKEYS, tn), lambda i: (0, 0, i))
    return pl.pallas_call(
        functools.partial(_retrieve_kernel, tn=tn, chunk=chunk),
        grid=(N // tn,),
        in_specs=[
            pl.BlockSpec((D_MODEL, tn), lambda i: (0, i)),
            pl.BlockSpec(wq_t.shape, lambda i: (0, 0), pipeline_mode=pl.Buffered(1)),
            pl.BlockSpec(keys_kh.shape, lambda i: (0, 0, 0), pipeline_mode=pl.Buffered(1)),
        ],
        out_specs=[out_spec] * 4,
        out_shape=[sds(jnp.uint32), sds(jnp.uint32), sds(_BF16), sds(_BF16)],
        scratch_shapes=[
            pltpu.VMEM((2, PEER_NKEYS * PEER_HEADS, tn), _F32),
            pltpu.VMEM((4, PEER_NKEYS * PEER_HEADS, chunk), _F32),
        ],
        compiler_params=pltpu.CompilerParams(
            dimension_semantics=("arbitrary",), vmem_limit_bytes=VMEM_LIMIT),
        name="peer_retrieve",
    )(h_t, wq_t, keys_kh)


def _experts_kernel(ht_ref, h1_ref, u_ref, vt_ref, cnt1_ref, e1_ref, rho2_ref, e2_ref, g_ref, b_ref,
                    out_ref, z0_buf, z1_buf, gz0_buf, gz1_buf, acc_buf, *, tn, te, n_e, lane_chunk):
    j = pl.program_id(0)
    rows = te // PEER_NKEYS
    e_prev = (j + n_e - 1) % n_e
    e_pp = (j + n_e - 2) % n_e

    @pl.when(j == 0)
    def _():
        z1_buf[...] = jnp.zeros_like(z1_buf)
        gz0_buf[...] = jnp.zeros_like(gz0_buf)
        gz1_buf[...] = jnp.zeros_like(gz1_buf)

    @pl.when(jnp.logical_or(e_pp == 0, j == 0))
    def _():
        acc_buf[...] = jnp.zeros_like(acc_buf)

    def stages(z_new, z_cur, gz_cur, gz_old):
        k1_base = pl.multiple_of(e_prev * rows, rows)
        zero = jnp.zeros((PACK, lane_chunk), _BF16)

        def weights(r_lo, r_hi):
            for c in range(tn // lane_chunk):
                ls = slice(c * lane_chunk, (c + 1) * lane_chunk)
                cnt1 = [cnt1_ref[h, pl.ds(k1_base, rows), ls] for h in range(PEER_HEADS)]
                e1 = [e1_ref[h, pl.ds(k1_base, rows), ls] for h in range(PEER_HEADS)]
                for r in range(r_lo, r_hi):
                    def bcast(w):
                        w8 = jnp.broadcast_to(w[r:r + 1, :], (SUBLANES, lane_chunk))
                        return pltpu.bitcast(w8, _BF16)
                    n_q = PEER_NKEYS // PACK
                    g = [None] * n_q
                    for h in range(PEER_HEADS):
                        cb = bcast(cnt1[h])
                        eb = bcast(e1[h])
                        for q in range(n_q):
                            ks = slice(q * PACK, (q + 1) * PACK)
                            m = rho2_ref[h, ks, ls] < cb
                            term = jnp.where(m, e2_ref[h, ks, ls], zero) * eb
                            g[q] = term if g[q] is None else g[q] + term
                    for q in range(n_q):
                        rs = slice(r * PEER_NKEYS + q * PACK, r * PEER_NKEYS + (q + 1) * PACK)
                        zz = z_cur[rs, ls].astype(_BF16)
                        gz_cur[rs, ls] = g[q] * _gelu_sig(zz)

        n_piece = 2 * MM_PIECES
        for p in range(n_piece):
            if p < MM_PIECES:
                ms = slice(p * te // MM_PIECES, (p + 1) * te // MM_PIECES)
                z_new[ms, :] = _dot(u_ref[ms, :], ht_ref[...])
            else:
                q = p - MM_PIECES
                ms = slice(q * D_MODEL // MM_PIECES, (q + 1) * D_MODEL // MM_PIECES)
                acc_buf[ms, :] += _dot(vt_ref[ms, :], gz_old[...])
            weights(p * rows // n_piece, (p + 1) * rows // n_piece)

    @pl.when(j % 2 == 0)
    def _():
        stages(z0_buf, z1_buf, gz1_buf, gz0_buf)

    @pl.when(j % 2 == 1)
    def _():
        stages(z1_buf, z0_buf, gz0_buf, gz1_buf)

    @pl.when(jnp.logical_and(e_pp == n_e - 1, j >= 2))
    def _():
        ffn = acc_buf[...].T
        out_ref[...] = _layer_norm(ALPHA * h1_ref[...] + ffn, g_ref[...], b_ref[...])


def _experts(h_t, h1, u_b, vt_b, cnt1, e1, rho2, e2, ln_g, ln_b, *, tn, te, lane_chunk=256):
    N = h1.shape[0]
    assert N % tn == 0 and N_EXPERTS % te == 0 and te % (SUBLANES * PEER_NKEYS) == 0
    n_e = N_EXPERTS // te
    total = (N // tn) * n_e
    last = total - 1
    tile0 = lambda j: jnp.minimum(j, last)
    tile1 = lambda j: jnp.clip(j - 1, 0, last)
    tile2 = lambda j: jnp.clip(j - 2, 0, last)
    sel_spec = pl.BlockSpec((PEER_HEADS, PEER_NKEYS, tn), lambda j: (0, 0, tile1(j) // n_e))
    return pl.pallas_call(
        functools.partial(_experts_kernel, tn=tn, te=te, n_e=n_e, lane_chunk=min(lane_chunk, tn)),
        grid=(total + 2,),
        in_specs=[
            pl.BlockSpec((D_MODEL, tn), lambda j: (0, tile0(j) // n_e)),
            pl.BlockSpec((tn, D_MODEL), lambda j: (tile2(j) // n_e, 0)),
            pl.BlockSpec((te, D_MODEL), lambda j: (tile0(j) % n_e, 0)),
            pl.BlockSpec((D_MODEL, te), lambda j: (0, tile2(j) % n_e)),
            sel_spec, sel_spec, sel_spec, sel_spec,
            pl.BlockSpec((1, D_MODEL), lambda j: (0, 0)),
            pl.BlockSpec((1, D_MODEL), lambda j: (0, 0)),
        ],
        out_specs=pl.BlockSpec((tn, D_MODEL), lambda j: (tile2(j) // n_e, 0)),
        out_shape=jax.ShapeDtypeStruct((N, D_MODEL), _F32),
        scratch_shapes=[
            pltpu.VMEM((te, tn), _F32),
            pltpu.VMEM((te, tn), _F32),
            pltpu.VMEM((te, tn), _BF16),
            pltpu.VMEM((te, tn), _BF16),
            pltpu.VMEM((D_MODEL, tn), _F32),
        ],
        compiler_params=pltpu.CompilerParams(
            dimension_semantics=("arbitrary",), vmem_limit_bytes=VMEM_LIMIT),
        name="peer_experts",
    )(h_t, h1, u_b, vt_b, cnt1, e1, rho2, e2, ln_g, ln_b)


def _pick_tile(n, target):
    t = min(n, target)
    while n % t:
        t //= 2
    return t


def kernel(x, meta, ln_in_g, ln_in_b, w_in, conv_w, conv_b, w_rg, b_rg, w_ig, b_ig, lru_L, w_proj_a, pool_w, pool_scale, w_proj_b, w_out, ln1_g, ln1_b, w_q, sub_keys, expert_u, expert_v, ln2_g, ln2_b):
    B, S, D = x.shape
    assert D == D_MODEL and w_in.shape[0] == DEPTH
    row = lambda v: v.reshape(1, -1).astype(_F32)
    l = 0
    mixer_weights = (
        row(ln_in_g), row(ln_in_b), w_in[l].astype(_BF16), conv_w[l].astype(_F32), row(conv_b[l]),
        jnp.concatenate([w_rg[l], w_ig[l]], axis=-1).astype(_BF16), row(b_rg[l]), row(b_ig[l]),
        row(lru_L[l]), w_proj_a[l].astype(_BF16), pool_w[l].astype(_BF16), row(pool_scale[l]),
        w_proj_b[l].astype(_BF16), w_out[l].astype(_BF16), row(ln1_g[l]), row(ln1_b[l]),
    )
    zero_state = (jnp.zeros((CONV_TAIL, D_RNN), _F32), jnp.zeros((POOL_TAIL, D_POOL), _F32),
                  jnp.zeros((SUBLANES, D_RNN), _F32))
    _, xr_t, xp_t, hs = _mixer(meta[None].astype(_F32), mixer_weights, zero_state,
                               tt=N_META, pos0=0, emit_state=True)
    (h1,) = _mixer(x, mixer_weights, (xr_t, xp_t, hs), tt=_pick_tile(S, 256), pos0=N_META,
                   emit_state=False)
    h1 = h1.reshape(B * S, D)
    N = B * S

    wq_t = w_q[l].reshape(D, PEER_HEADS, 2, PEER_HALF).transpose(2, 1, 3, 0)
    wq_t = wq_t.reshape(2 * PEER_HEADS * PEER_HALF, D).astype(_BF16)
    eye = jnp.eye(PEER_HEADS, dtype=_F32)
    keys_kh = jnp.einsum("hpkd,hg->pkhgd", sub_keys[l], eye)
    keys_kh = keys_kh.reshape(2, PEER_NKEYS * PEER_HEADS, PEER_HEADS * PEER_HALF).astype(_BF16)
    tn = _pick_tile(N, 512)
    h_t = h1.astype(_BF16).T
    cnt1, e1, rho2, e2 = _retrieve(h_t, wq_t, keys_kh, tn=tn)

    u_b = expert_u[l].astype(_BF16)
    vt_b = expert_v[l].astype(_BF16).T
    out = _experts(h_t, h1, u_b, vt_b, cnt1, e1, rho2, e2, row(ln2_g[l]), row(ln2_b[l]),
                   tn=tn, te=1024)
    return out.reshape(B, S, D).astype(x.dtype)
```

```python
import functools

import jax
import jax.numpy as jnp
from jax import lax
from jax.experimental import pallas as pl
from jax.experimental.pallas import tpu as pltpu

D_MODEL = 1024
N_META = 16
D_RNN = D_MODEL
N_RNN_BLOCKS = 4
RNN_BLOCK = D_RNN // N_RNN_BLOCKS
CONV_WIDTH = 4
LRU_C = 8.0
POOL_WINDOWS = (2, 4, 8, 16)
D_POOL = D_MODEL // 2
POOL_GROUP = D_POOL // len(POOL_WINDOWS)
PEER_HEADS = 8
PEER_NKEYS = 128
N_EXPERTS = PEER_NKEYS * PEER_NKEYS
PEER_HALF = 128
PEER_TOPK = 16
DEPTH = 1
ALPHA = (2.0 * DEPTH) ** 0.25
LN_EPS = 1e-5

SUBLANES = 8
PACK = 16
KEY_UNROLL = 4
LANES = 128
CONV_TAIL = SUBLANES
POOL_TAIL = 16
VMEM_LIMIT = 56 * 1024 * 1024

_F32 = jnp.float32
_BF16 = jnp.bfloat16


def _layer_norm(v, g, b):
    mu = jnp.mean(v, axis=-1, keepdims=True)
    vc = v - mu
    var = jnp.mean(vc * vc, axis=-1, keepdims=True)
    return vc * lax.rsqrt(var + LN_EPS) * g + b


def _gelu_tanh(v):
    return 0.5 * v * (1.0 + jnp.tanh(0.7978845608028654 * (v + 0.044715 * (v * v * v))))


def _gelu_sig(v):
    w = v * (-2.3022082870680315 - 0.10294324120074478 * (v * v))
    return v / (1.0 + jnp.exp2(w))


def _sigmoid(v):
    return 1.0 / (1.0 + jnp.exp(-v))


def _dot(a, b):
    return jnp.dot(a, b, preferred_element_type=_F32)


def _lru_scan(a, b, h_prev):
    tt, c = a.shape
    groups = tt // SUBLANES
    a3 = a.reshape(groups, SUBLANES, c)
    b3 = b.reshape(groups, SUBLANES, c)
    row = lax.broadcasted_iota(jnp.int32, a3.shape, 1)
    shift = 1
    while shift < SUBLANES:
        a_sh = pltpu.roll(a3, shift, axis=1)
        b_sh = pltpu.roll(b3, shift, axis=1)
        m = row >= shift
        b3 = jnp.where(m, a3 * b_sh + b3, b3)
        a3 = jnp.where(m, a3 * a_sh, a3)
        shift *= 2
    outs = []
    carry = h_prev
    for g in range(groups):
        hg = a3[g] * carry + b3[g]
        outs.append(hg)
        carry = hg[SUBLANES - 1:SUBLANES, :]
    return jnp.concatenate(outs, axis=0)


def _mixer_kernel(x_ref, lng_ref, lnb_ref, win_ref, convw_ref, convb_ref, wgate_ref, brg_ref,
                  big_ref, lrul_ref, wpa_ref, poolw_ref, pscale_ref, wpb_ref, wout_ref,
                  ln1g_ref, ln1b_ref, xr0_ref, xp0_ref, hs0_ref,
                  h1_ref, *rest, tt, pos0, emit_state):
    if emit_state:
        xr_out, xp_out, hs_out, xr_buf, xp_buf, hs_buf, rg_buf, mx_buf = rest
    else:
        ht_ref, xr_buf, xp_buf, hs_buf, rg_buf, mx_buf = rest
    t = pl.program_id(1)

    @pl.when(t == 0)
    def _():
        xr_buf[0:CONV_TAIL, :] = xr0_ref[...]
        xp_buf[0:POOL_TAIL, :] = xp0_ref[...]
        hs_buf[...] = hs0_ref[...]

    x = x_ref[0]
    h0 = _layer_norm(x, lng_ref[...], lnb_ref[...])
    h0b = h0.astype(_BF16)

    xr_buf[CONV_TAIL:CONV_TAIL + tt, :] = _dot(h0b, win_ref[:, 0:D_RNN])
    lsig = lrul_ref[...]
    log_sig = jnp.minimum(lsig, 0.0) - jnp.log1p(jnp.exp(-jnp.abs(lsig)))
    for n in range(N_RNN_BLOCKS):
        cs = slice(n * RNN_BLOCK, (n + 1) * RNN_BLOCK)
        xc = convb_ref[:, cs]
        for j in range(CONV_WIDTH):
            off = CONV_TAIL - (CONV_WIDTH - 1) + j
            xc = xc + convw_ref[j:j + 1, cs] * xr_buf[off:off + tt, cs]
        pre = _dot(xc.astype(_BF16), wgate_ref[n])
        r = _sigmoid(pre[:, :RNN_BLOCK] + brg_ref[:, cs])
        i = _sigmoid(pre[:, RNN_BLOCK:] + big_ref[:, cs])
        log_a = (LRU_C * log_sig[:, cs]) * r
        a = jnp.exp(log_a)
        mult = jnp.sqrt(1.0 - a * a)
        h = _lru_scan(a, mult * i * xc, hs_buf[0:1, cs])
        hs_buf[:, cs] = jnp.broadcast_to(h[tt - 1:tt, :], (SUBLANES, RNN_BLOCK))
        gate = _dot(h0b, win_ref[:, D_RNN + n * RNN_BLOCK:D_RNN + (n + 1) * RNN_BLOCK])
        rg_buf[:, cs] = (h * _gelu_tanh(gate)).astype(_BF16)
    y_a = _dot(rg_buf[...], wpa_ref[...])

    xp_buf[POOL_TAIL:POOL_TAIL + tt, :] = _dot(h0b, win_ref[:, 2 * D_RNN:2 * D_RNN + D_POOL])
    if pos0 < max(POOL_WINDOWS):
        pos = (lax.broadcasted_iota(jnp.int32, (tt, 1), 0) + (pos0 + 1) + t * tt).astype(_F32)
    for g, w in enumerate(POOL_WINDOWS):
        cs = slice(g * POOL_GROUP, (g + 1) * POOL_GROUP)
        cur = xp_buf[POOL_TAIL:POOL_TAIL + tt, cs]
        acc = cur
        for j in range(1, w):
            acc = acc + xp_buf[POOL_TAIL - j:POOL_TAIL - j + tt, cs]
        if pos0 < max(POOL_WINDOWS):
            mean = acc / jnp.minimum(pos, float(w))
        else:
            mean = acc * (1.0 / w)
        mixed = _dot((mean - cur).astype(_BF16), poolw_ref[g]) * pscale_ref[:, cs]
        mx_buf[:, cs] = mixed.astype(_BF16)
    y_b = _dot(mx_buf[...], wpb_ref[...])

    g_off = 2 * D_RNN + D_POOL
    gate_a = _sigmoid(_dot(h0b, win_ref[:, g_off:g_off + D_MODEL]))
    gate_b = _sigmoid(_dot(h0b, win_ref[:, g_off + D_MODEL:g_off + 2 * D_MODEL]))
    merged = gate_a * y_a + gate_b * y_b
    mix = _dot(merged.astype(_BF16), wout_ref[...])
    h1 = _layer_norm(ALPHA * h0 + mix, ln1g_ref[...], ln1b_ref[...])
    h1_ref[0] = h1
    if not emit_state:
        ht_ref[...] = h1.T.astype(_BF16)

    xr_buf[0:CONV_TAIL, :] = xr_buf[tt:tt + CONV_TAIL, :]
    xp_buf[0:POOL_TAIL, :] = xp_buf[tt:tt + POOL_TAIL, :]
    if emit_state:
        xr_out[...] = xr_buf[0:CONV_TAIL, :]
        xp_out[...] = xp_buf[0:POOL_TAIL, :]
        hs_out[...] = hs_buf[...]


def _const_spec(shape):
    nd = len(shape)
    return pl.BlockSpec(shape, lambda b, t: (0,) * nd, pipeline_mode=pl.Buffered(1))


def _mixer(x, weights, state, *, tt, pos0, emit_state):
    B, S, D = x.shape
    assert S % tt == 0 and tt % POOL_TAIL == 0
    in_specs = [pl.BlockSpec((1, tt, D), lambda b, t: (b, t, 0))]
    in_specs += [_const_spec(w.shape) for w in weights]
    in_specs += [_const_spec(s.shape) for s in state]
    out_shape = [jax.ShapeDtypeStruct((B, S, D), _F32)]
    out_specs = [pl.BlockSpec((1, tt, D), lambda b, t: (b, t, 0))]
    if emit_state:
        assert B == 1 and S == tt
        out_shape += [jax.ShapeDtypeStruct(s.shape, _F32) for s in state]
        out_specs += [pl.BlockSpec(s.shape, lambda b, t: (0, 0)) for s in state]
    else:
        assert tt % LANES == 0
        n_t = S // tt
        out_shape += [jax.ShapeDtypeStruct((D, B * S), _BF16)]
        out_specs += [pl.BlockSpec((D, tt), lambda b, t: (0, b * n_t + t))]
    scratch = [
        pltpu.VMEM((CONV_TAIL + tt, D_RNN), _F32),
        pltpu.VMEM((POOL_TAIL + tt, D_POOL), _F32),
        pltpu.VMEM((SUBLANES, D_RNN), _F32),
        pltpu.VMEM((tt, D_RNN), _BF16),
        pltpu.VMEM((tt, D_POOL), _BF16),
    ]
    return pl.pallas_call(
        functools.partial(_mixer_kernel, tt=tt, pos0=pos0, emit_state=emit_state),
        grid=(B, S // tt),
        in_specs=in_specs,
        out_specs=out_specs,
        out_shape=out_shape,
        scratch_shapes=scratch,
        compiler_params=pltpu.CompilerParams(
            dimension_semantics=("arbitrary", "arbitrary"), vmem_limit_bytes=VMEM_LIMIT),
        name="mixer_meta" if emit_state else "mixer",
    )(x, *weights, *state)


def _oddeven_merge_sort_pairs(n):
    pairs = []

    def merge(lo, hi, r):
        step = r * 2
        if step < hi - lo:
            merge(lo, hi, step)
            merge(lo + r, hi, step)
            for i in range(lo + r, hi - r, step):
                pairs.append((i, i + r))
        else:
            pairs.append((lo, lo + r))

    def sort(lo, hi):
        if hi - lo >= 1:
            mid = lo + (hi - lo) // 2
            sort(lo, mid)
            sort(mid + 1, hi)
            merge(lo, hi, 1)

    sort(0, n - 1)
    return pairs


_SORT16 = _oddeven_merge_sort_pairs(PEER_TOPK)


def _cmpx(v, i, j):
    hi = jnp.maximum(v[i], v[j])
    lo = jnp.minimum(v[i], v[j])
    v[i] = hi
    v[j] = lo


def _sort16_desc(v):
    v = list(v)
    for i, j in _SORT16:
        _cmpx(v, i, j)
    return v


def _merge_top16(a, b):
    c = list(a)
    for k in range(len(b)):
        c[PEER_TOPK - 1 - k] = jnp.maximum(a[PEER_TOPK - 1 - k], b[k])
    stride = PEER_TOPK // 2
    while stride >= 1:
        for i in range(PEER_TOPK):
            if i & stride == 0:
                _cmpx(c, i, i + stride)
        stride //= 2
    return c


def _top16_desc(vals):
    runs = [_sort16_desc(vals[i:i + PEER_TOPK]) for i in range(0, len(vals), PEER_TOPK)]
    while len(runs) > 1:
        runs = [_merge_top16(runs[i], runs[i + 1]) for i in range(0, len(runs), 2)]
    return runs[0]


_CAND = [(i, j) for i in range(PEER_TOPK) for j in range(PEER_TOPK) if (i + 1) * (j + 1) <= PEER_TOPK]


def _dup_bf16_words(v):
    bits = lax.bitcast_convert_type(v.astype(_BF16).astype(_F32), jnp.uint32)
    return bits | (bits >> 16)


def _retrieve_kernel(ht_ref, wq_ref, keys_ref, cnt1_ref, e1_ref, rho2_ref, e2_ref,
                     s_buf, x_buf, *, tn, chunk):
    q_t = _dot(wq_ref[...], ht_ref[...]).astype(_BF16)
    half_rows = PEER_HEADS * PEER_HALF
    for p in range(2):
        s_buf[p] = _dot(keys_ref[p], q_t[p * half_rows:(p + 1) * half_rows, :])

    neg = jnp.full((PEER_HEADS, chunk), -jnp.inf, _F32)

    def slab(k):
        return pl.ds(pl.multiple_of(k * PEER_HEADS, PEER_HEADS), PEER_HEADS)

    def chunk_body(c, _):
        ls = pl.ds(pl.multiple_of(c * chunk, chunk), chunk)
        a = _top16_desc([s_buf[0, k * PEER_HEADS:(k + 1) * PEER_HEADS, ls] for k in range(PEER_NKEYS)])
        b = _top16_desc([s_buf[1, k * PEER_HEADS:(k + 1) * PEER_HEADS, ls] for k in range(PEER_NKEYS)])
        rows = {}
        for i, j in _CAND:
            rows.setdefault(i, []).append(a[i] + b[j])
        top = rows[0]
        rest = [v for i in range(1, PEER_TOPK) for v in rows[i]]
        while rest:
            grp, rest = rest[:PEER_TOPK], rest[PEER_TOPK:]
            grp = grp + [neg] * (PEER_TOPK - len(grp))
            top = _merge_top16(top, _sort16_desc(grp))
        c16 = top[PEER_TOPK - 1]
        ea = [jnp.exp(a[i] - a[0]) for i in range(PEER_TOPK)]
        eb = [jnp.exp(b[j] - b[0]) for j in range(PEER_TOPK)]
        z = jnp.zeros((PEER_HEADS, chunk), _F32)
        for i, j in _CAND:
            z = z + jnp.where(a[i] + b[j] >= c16, ea[i] * eb[j], 0.0)
        zinv = 1.0 / z

        def key_body(kk, _):
            for u in range(KEY_UNROLL):
                k = kk * KEY_UNROLL + u
                s1 = s_buf[0, slab(k), ls]
                s2 = s_buf[1, slab(k), ls]
                cnt = jnp.zeros((PEER_HEADS, chunk), _F32)
                rho = jnp.zeros((PEER_HEADS, chunk), _F32)
                for j in range(PEER_TOPK):
                    cnt = jnp.where(s1 + b[j] >= c16, j + 1.0, cnt)
                    rho = jnp.where(b[j] > s2, j + 1.0, rho)
                x_buf[0, slab(k), :] = lax.bitcast_convert_type(_dup_bf16_words(cnt), _F32)
                x_buf[1, slab(k), :] = lax.bitcast_convert_type(
                    _dup_bf16_words(jnp.exp(s1 - a[0]) * zinv), _F32)
                x_buf[2, slab(k), :] = rho
                x_buf[3, slab(k), :] = jnp.exp(s2 - b[0])
            return 0

        lax.fori_loop(0, PEER_NKEYS // KEY_UNROLL, key_body, 0)

        for h in range(PEER_HEADS):
            rows_h = pl.ds(h, PEER_NKEYS, stride=PEER_HEADS)
            cnt1_ref[h, :, ls] = lax.bitcast_convert_type(x_buf[0, rows_h, :], jnp.uint32)
            e1_ref[h, :, ls] = lax.bitcast_convert_type(x_buf[1, rows_h, :], jnp.uint32)
            rho2_ref[h, :, ls] = x_buf[2, rows_h, :].astype(_BF16)
            e2_ref[h, :, ls] = x_buf[3, rows_h, :].astype(_BF16)
        return 0

    lax.fori_loop(0, tn // chunk, chunk_body, 0)


def _retrieve(h_t, wq_t, keys_kh, *, tn, chunk=LANES):
    N = h_t.shape[1]
    assert N % tn == 0 and tn % chunk == 0 and chunk == LANES
    sds = lambda dt: jax.ShapeDtypeStruct((PEER_HEADS, PEER_NKEYS, N), dt)
    out_spec = pl.BlockSpec((PEER_HEADS, PEER_NKEYS, tn), lambda i: (0, 0, i))
    return pl.pallas_call(
        functools.partial(_retrieve_kernel, tn=tn, chunk=chunk),
        grid=(N // tn,),
        in_specs=[
            pl.BlockSpec((D_MODEL, tn), lambda i: (0, i)),
            pl.BlockSpec(wq_t.shape, lambda i: (0, 0), pipeline_mode=pl.Buffered(1)),
            pl.BlockSpec(keys_kh.shape, lambda i: (0, 0, 0), pipeline_mode=pl.Buffered(1)),
        ],
        out_specs=[out_spec] * 4,
        out_shape=[sds(jnp.uint32), sds(jnp.uint32), sds(_BF16), sds(_BF16)],
        scratch_shapes=[
            pltpu.VMEM((2, PEER_NKEYS * PEER_HEADS, tn), _F32),
            pltpu.VMEM((4, PEER_NKEYS * PEER_HEADS, chunk), _F32),
        ],
        compiler_params=pltpu.CompilerParams(
            dimension_semantics=("arbitrary",), vmem_limit_bytes=VMEM_LIMIT),
        name="peer_retrieve",
    )(h_t, wq_t, keys_kh)


def _experts_kernel(ht_ref, h1_ref, u_ref, vt_ref, cnt1_ref, e1_ref, rho2_ref, e2_ref, g_ref, b_ref,
                    out_ref, z0_buf, z1_buf, gz0_buf, gz1_buf, acc_buf, *, tn, te, n_e, lane_chunk):
    j = pl.program_id(0)
    rows = te // PEER_NKEYS
    e_prev = (j + n_e - 1) % n_e
    e_pp = (j + n_e - 2) % n_e

    @pl.when(j == 0)
    def _():
        z1_buf[...] = jnp.zeros_like(z1_buf)
        gz0_buf[...] = jnp.zeros_like(gz0_buf)
        gz1_buf[...] = jnp.zeros_like(gz1_buf)

    @pl.when(jnp.logical_or(e_pp == 0, j == 0))
    def _():
        acc_buf[...] = jnp.zeros_like(acc_buf)

    def stages(z_new, z_cur, gz_cur, gz_old):
        k1_base = pl.multiple_of(e_prev * rows, rows)
        zero = jnp.zeros((PACK, lane_chunk), _BF16)

        def weights(r_lo, r_hi):
            for c in range(tn // lane_chunk):
                ls = slice(c * lane_chunk, (c + 1) * lane_chunk)
                cnt1 = [cnt1_ref[h, pl.ds(k1_base, rows), ls] for h in range(PEER_HEADS)]
                e1 = [e1_ref[h, pl.ds(k1_base, rows), ls] for h in range(PEER_HEADS)]
                for r in range(r_lo, r_hi):
                    def bcast(w):
                        w8 = jnp.broadcast_to(w[r:r + 1, :], (SUBLANES, lane_chunk))
                        return pltpu.bitcast(w8, _BF16)
                    n_q = PEER_NKEYS // PACK
                    g = [None] * n_q
                    for h in range(PEER_HEADS):
                        cb = bcast(cnt1[h])
                        eb = bcast(e1[h])
                        for q in range(n_q):
                            ks = slice(q * PACK, (q + 1) * PACK)
                            m = rho2_ref[h, ks, ls] < cb
                            term = jnp.where(m, e2_ref[h, ks, ls], zero) * eb
                            g[q] = term if g[q] is None else g[q] + term
                    for q in range(n_q):
                        rs = slice(r * PEER_NKEYS + q * PACK, r * PEER_NKEYS + (q + 1) * PACK)
                        zz = z_cur[rs, ls].astype(_BF16)
                        gz_cur[rs, ls] = g[q] * _gelu_sig(zz)

        acc_buf[...] += _dot(vt_ref[...], gz_old[...])
        z_new[...] = _dot(u_ref[...], ht_ref[...])
        weights(0, rows)

    @pl.when(j % 2 == 0)
    def _():
        stages(z0_buf, z1_buf, gz1_buf, gz0_buf)

    @pl.when(j % 2 == 1)
    def _():
        stages(z1_buf, z0_buf, gz0_buf, gz1_buf)

    @pl.when(jnp.logical_and(e_pp == n_e - 1, j >= 2))
    def _():
        ffn = acc_buf[...].T
        out_ref[...] = _layer_norm(ALPHA * h1_ref[...] + ffn, g_ref[...], b_ref[...])


def _experts(h_t, h1, u_b, vt_b, cnt1, e1, rho2, e2, ln_g, ln_b, *, tn, te, lane_chunk=256):
    N = h1.shape[0]
    assert N % tn == 0 and N_EXPERTS % te == 0 and te % (SUBLANES * PEER_NKEYS) == 0
    n_e = N_EXPERTS // te
    total = (N // tn) * n_e
    last = total - 1
    tile0 = lambda j: jnp.minimum(j, last)
    tile1 = lambda j: jnp.clip(j - 1, 0, last)
    tile2 = lambda j: jnp.clip(j - 2, 0, last)
    sel_spec = pl.BlockSpec((PEER_HEADS, PEER_NKEYS, tn), lambda j: (0, 0, tile1(j) // n_e))
    return pl.pallas_call(
        functools.partial(_experts_kernel, tn=tn, te=te, n_e=n_e, lane_chunk=min(lane_chunk, tn)),
        grid=(total + 2,),
        in_specs=[
            pl.BlockSpec((D_MODEL, tn), lambda j: (0, tile0(j) // n_e)),
            pl.BlockSpec((tn, D_MODEL), lambda j: (tile2(j) // n_e, 0)),
            pl.BlockSpec((te, D_MODEL), lambda j: (tile0(j) % n_e, 0)),
            pl.BlockSpec((D_MODEL, te), lambda j: (0, tile2(j) % n_e)),
            sel_spec, sel_spec, sel_spec, sel_spec,
            pl.BlockSpec((1, D_MODEL), lambda j: (0, 0)),
            pl.BlockSpec((1, D_MODEL), lambda j: (0, 0)),
        ],
        out_specs=pl.BlockSpec((tn, D_MODEL), lambda j: (tile2(j) // n_e, 0)),
        out_shape=jax.ShapeDtypeStruct((N, D_MODEL), _F32),
        scratch_shapes=[
            pltpu.VMEM((te, tn), _F32),
            pltpu.VMEM((te, tn), _F32),
            pltpu.VMEM((te, tn), _BF16),
            pltpu.VMEM((te, tn), _BF16),
            pltpu.VMEM((D_MODEL, tn), _F32),
        ],
        compiler_params=pltpu.CompilerParams(
            dimension_semantics=("arbitrary",), vmem_limit_bytes=VMEM_LIMIT),
        name="peer_experts",
    )(h_t, h1, u_b, vt_b, cnt1, e1, rho2, e2, ln_g, ln_b)


def _pick_tile(n, target):
    t = min(n, target)
    while n % t:
        t //= 2
    return t


def kernel(x, meta, ln_in_g, ln_in_b, w_in, conv_w, conv_b, w_rg, b_rg, w_ig, b_ig, lru_L, w_proj_a, pool_w, pool_scale, w_proj_b, w_out, ln1_g, ln1_b, w_q, sub_keys, expert_u, expert_v, ln2_g, ln2_b):
    B, S, D = x.shape
    assert D == D_MODEL and w_in.shape[0] == DEPTH
    row = lambda v: v.reshape(1, -1).astype(_F32)
    l = 0
    mixer_weights = (
        row(ln_in_g), row(ln_in_b), w_in[l].astype(_BF16), conv_w[l].astype(_F32), row(conv_b[l]),
        jnp.concatenate([w_rg[l], w_ig[l]], axis=-1).astype(_BF16), row(b_rg[l]), row(b_ig[l]),
        row(lru_L[l]), w_proj_a[l].astype(_BF16), pool_w[l].astype(_BF16), row(pool_scale[l]),
        w_proj_b[l].astype(_BF16), w_out[l].astype(_BF16), row(ln1_g[l]), row(ln1_b[l]),
    )
    zero_state = (jnp.zeros((CONV_TAIL, D_RNN), _F32), jnp.zeros((POOL_TAIL, D_POOL), _F32),
                  jnp.zeros((SUBLANES, D_RNN), _F32))
    _, xr_t, xp_t, hs = _mixer(meta[None].astype(_F32), mixer_weights, zero_state,
                               tt=N_META, pos0=0, emit_state=True)
    h1, h_t = _mixer(x, mixer_weights, (xr_t, xp_t, hs), tt=_pick_tile(S, 256), pos0=N_META,
                     emit_state=False)
    h1 = h1.reshape(B * S, D)
    N = B * S

    wq_t = w_q[l].reshape(D, PEER_HEADS, 2, PEER_HALF).transpose(2, 1, 3, 0)
    wq_t = wq_t.reshape(2 * PEER_HEADS * PEER_HALF, D).astype(_BF16)
    eye = jnp.eye(PEER_HEADS, dtype=_F32)
    keys_kh = jnp.einsum("hpkd,hg->pkhgd", sub_keys[l], eye)
    keys_kh = keys_kh.reshape(2, PEER_NKEYS * PEER_HEADS, PEER_HEADS * PEER_HALF).astype(_BF16)
    tn = _pick_tile(N, 512)
    cnt1, e1, rho2, e2 = _retrieve(h_t, wq_t, keys_kh, tn=tn)

    u_b = expert_u[l].astype(_BF16)
    vt_b = expert_v[l].astype(_BF16).T
    out = _experts(h_t, h1, u_b, vt_b, cnt1, e1, rho2, e2, row(ln2_g[l]), row(ln2_b[l]),
                   tn=tn, te=1024)
    return out.reshape(B, S, D).astype(x.dtype)
```

```python
import functools

import jax
import jax.numpy as jnp
from jax import lax
from jax.experimental import pallas as pl
from jax.experimental.pallas import tpu as pltpu

D_MODEL = 1024
N_META = 16
D_RNN = D_MODEL
N_RNN_BLOCKS = 4
RNN_BLOCK = D_RNN // N_RNN_BLOCKS
CONV_WIDTH = 4
LRU_C = 8.0
POOL_WINDOWS = (2, 4, 8, 16)
D_POOL = D_MODEL // 2
POOL_GROUP = D_POOL // len(POOL_WINDOWS)
PEER_HEADS = 8
PEER_NKEYS = 128
N_EXPERTS = PEER_NKEYS * PEER_NKEYS
PEER_HALF = 128
PEER_TOPK = 16
DEPTH = 1
ALPHA = (2.0 * DEPTH) ** 0.25
LN_EPS = 1e-5

SUBLANES = 8
PACK = 16
KEY_UNROLL = 4
LANES = 128
CONV_TAIL = SUBLANES
POOL_TAIL = 16
VMEM_LIMIT = 56 * 1024 * 1024

_F32 = jnp.float32
_BF16 = jnp.bfloat16


def _layer_norm(v, g, b):
    mu = jnp.mean(v, axis=-1, keepdims=True)
    vc = v - mu
    var = jnp.mean(vc * vc, axis=-1, keepdims=True)
    return vc * lax.rsqrt(var + LN_EPS) * g + b


def _gelu_tanh(v):
    return 0.5 * v * (1.0 + jnp.tanh(0.7978845608028654 * (v + 0.044715 * (v * v * v))))


def _gelu_sig(v):
    w = v * (-2.3022082870680315 - 0.10294324120074478 * (v * v))
    return v / (1.0 + jnp.exp2(w))


def _sigmoid(v):
    return 1.0 / (1.0 + jnp.exp(-v))


def _dot(a, b):
    return jnp.dot(a, b, preferred_element_type=_F32)


def _dot_f32_lhs(a, b):
    return lax.dot_general(a, b, (((1,), (0,)), ((), ())), preferred_element_type=_F32)


def _lru_scan(a, b, h_prev):
    tt, c = a.shape
    groups = tt // SUBLANES
    a3 = a.reshape(groups, SUBLANES, c)
    b3 = b.reshape(groups, SUBLANES, c)
    row = lax.broadcasted_iota(jnp.int32, a3.shape, 1)
    shift = 1
    while shift < SUBLANES:
        a_sh = pltpu.roll(a3, shift, axis=1)
        b_sh = pltpu.roll(b3, shift, axis=1)
        m = row >= shift
        b3 = jnp.where(m, a3 * b_sh + b3, b3)
        a3 = jnp.where(m, a3 * a_sh, a3)
        shift *= 2
    outs = []
    carry = h_prev
    for g in range(groups):
        hg = a3[g] * carry + b3[g]
        outs.append(hg)
        carry = hg[SUBLANES - 1:SUBLANES, :]
    return jnp.concatenate(outs, axis=0)


def _mixer_kernel(x_ref, lng_ref, lnb_ref, win_ref, convw_ref, convb_ref, wgate_ref, brg_ref,
                  big_ref, lrul_ref, wpa_ref, poolw_ref, pscale_ref, wpb_ref, wout_ref,
                  ln1g_ref, ln1b_ref, xr0_ref, xp0_ref, hs0_ref,
                  h1_ref, *rest, tt, pos0, emit_state):
    if emit_state:
        xr_out, xp_out, hs_out, xr_buf, xp_buf, hs_buf, rg_buf, mx_buf = rest
    else:
        ht_ref, xr_buf, xp_buf, hs_buf, rg_buf, mx_buf = rest
    t = pl.program_id(1)

    @pl.when(t == 0)
    def _():
        xr_buf[0:CONV_TAIL, :] = xr0_ref[...]
        xp_buf[0:POOL_TAIL, :] = xp0_ref[...]
        hs_buf[...] = hs0_ref[...]

    x = x_ref[0]
    h0 = _layer_norm(x, lng_ref[...], lnb_ref[...])
    h0b = h0.astype(_BF16)

    xr_buf[CONV_TAIL:CONV_TAIL + tt, :] = _dot(h0b, win_ref[:, 0:D_RNN])
    lsig = lrul_ref[...]
    log_sig = jnp.minimum(lsig, 0.0) - jnp.log1p(jnp.exp(-jnp.abs(lsig)))
    for n in range(N_RNN_BLOCKS):
        cs = slice(n * RNN_BLOCK, (n + 1) * RNN_BLOCK)
        xc = convb_ref[:, cs]
        for j in range(CONV_WIDTH):
            off = CONV_TAIL - (CONV_WIDTH - 1) + j
            xc = xc + convw_ref[j:j + 1, cs] * xr_buf[off:off + tt, cs]
        pre = _dot(xc.astype(_BF16), wgate_ref[n])
        r = _sigmoid(pre[:, :RNN_BLOCK] + brg_ref[:, cs])
        i = _sigmoid(pre[:, RNN_BLOCK:] + big_ref[:, cs])
        log_a = (LRU_C * log_sig[:, cs]) * r
        a = jnp.exp(log_a)
        mult = jnp.sqrt(1.0 - a * a)
        h = _lru_scan(a, mult * i * xc, hs_buf[0:1, cs])
        hs_buf[:, cs] = jnp.broadcast_to(h[tt - 1:tt, :], (SUBLANES, RNN_BLOCK))
        gate = _dot(h0b, win_ref[:, D_RNN + n * RNN_BLOCK:D_RNN + (n + 1) * RNN_BLOCK])
        rg_buf[:, cs] = (h * _gelu_tanh(gate)).astype(_BF16)
    y_a = _dot(rg_buf[...], wpa_ref[...])

    xp_buf[POOL_TAIL:POOL_TAIL + tt, :] = _dot(h0b, win_ref[:, 2 * D_RNN:2 * D_RNN + D_POOL])
    if pos0 < max(POOL_WINDOWS):
        pos = (lax.broadcasted_iota(jnp.int32, (tt, 1), 0) + (pos0 + 1) + t * tt).astype(_F32)
    for g, w in enumerate(POOL_WINDOWS):
        cs = slice(g * POOL_GROUP, (g + 1) * POOL_GROUP)
        cur = xp_buf[POOL_TAIL:POOL_TAIL + tt, cs]
        acc = cur
        for j in range(1, w):
            acc = acc + xp_buf[POOL_TAIL - j:POOL_TAIL - j + tt, cs]
        if pos0 < max(POOL_WINDOWS):
            mean = acc / jnp.minimum(pos, float(w))
        else:
            mean = acc * (1.0 / w)
        mixed = _dot((mean - cur).astype(_BF16), poolw_ref[g]) * pscale_ref[:, cs]
        mx_buf[:, cs] = mixed.astype(_BF16)
    y_b = _dot(mx_buf[...], wpb_ref[...])

    g_off = 2 * D_RNN + D_POOL
    gate_a = _sigmoid(_dot(h0b, win_ref[:, g_off:g_off + D_MODEL]))
    gate_b = _sigmoid(_dot(h0b, win_ref[:, g_off + D_MODEL:g_off + 2 * D_MODEL]))
    merged = gate_a * y_a + gate_b * y_b
    mix = _dot(merged.astype(_BF16), wout_ref[...])
    h1 = _layer_norm(ALPHA * h0 + mix, ln1g_ref[...], ln1b_ref[...])
    h1_ref[0] = h1
    if not emit_state:
        ht_ref[...] = h1.T.astype(_BF16)

    xr_buf[0:CONV_TAIL, :] = xr_buf[tt:tt + CONV_TAIL, :]
    xp_buf[0:POOL_TAIL, :] = xp_buf[tt:tt + POOL_TAIL, :]
    if emit_state:
        xr_out[...] = xr_buf[0:CONV_TAIL, :]
        xp_out[...] = xp_buf[0:POOL_TAIL, :]
        hs_out[...] = hs_buf[...]


def _const_spec(shape):
    nd = len(shape)
    return pl.BlockSpec(shape, lambda b, t: (0,) * nd, pipeline_mode=pl.Buffered(1))


def _mixer(x, weights, state, *, tt, pos0, emit_state):
    B, S, D = x.shape
    assert S % tt == 0 and tt % POOL_TAIL == 0
    in_specs = [pl.BlockSpec((1, tt, D), lambda b, t: (b, t, 0))]
    in_specs += [_const_spec(w.shape) for w in weights]
    in_specs += [_const_spec(s.shape) for s in state]
    out_shape = [jax.ShapeDtypeStruct((B, S, D), _F32)]
    out_specs = [pl.BlockSpec((1, tt, D), lambda b, t: (b, t, 0))]
    if emit_state:
        assert B == 1 and S == tt
        out_shape += [jax.ShapeDtypeStruct(s.shape, _F32) for s in state]
        out_specs += [pl.BlockSpec(s.shape, lambda b, t: (0, 0)) for s in state]
    else:
        assert tt % LANES == 0
        n_t = S // tt
        out_shape += [jax.ShapeDtypeStruct((D, B * S), _BF16)]
        out_specs += [pl.BlockSpec((D, tt), lambda b, t: (0, b * n_t + t))]
    scratch = [
        pltpu.VMEM((CONV_TAIL + tt, D_RNN), _F32),
        pltpu.VMEM((POOL_TAIL + tt, D_POOL), _F32),
        pltpu.VMEM((SUBLANES, D_RNN), _F32),
        pltpu.VMEM((tt, D_RNN), _BF16),
        pltpu.VMEM((tt, D_POOL), _BF16),
    ]
    return pl.pallas_call(
        functools.partial(_mixer_kernel, tt=tt, pos0=pos0, emit_state=emit_state),
        grid=(B, S // tt),
        in_specs=in_specs,
        out_specs=out_specs,
        out_shape=out_shape,
        scratch_shapes=scratch,
        compiler_params=pltpu.CompilerParams(
            dimension_semantics=("arbitrary", "arbitrary"), vmem_limit_bytes=VMEM_LIMIT),
        name="mixer_meta" if emit_state else "mixer",
    )(x, *weights, *state)


def _oddeven_merge_sort_pairs(n):
    pairs = []

    def merge(lo, hi, r):
        step = r * 2
        if step < hi - lo:
            merge(lo, hi, step)
            merge(lo + r, hi, step)
            for i in range(lo + r, hi - r, step):
                pairs.append((i, i + r))
        else:
            pairs.append((lo, lo + r))

    def sort(lo, hi):
        if hi - lo >= 1:
            mid = lo + (hi - lo) // 2
            sort(lo, mid)
            sort(mid + 1, hi)
            merge(lo, hi, 1)

    sort(0, n - 1)
    return pairs


_SORT16 = _oddeven_merge_sort_pairs(PEER_TOPK)


def _cmpx(v, i, j):
    hi = jnp.maximum(v[i], v[j])
    lo = jnp.minimum(v[i], v[j])
    v[i] = hi
    v[j] = lo


def _sort16_desc(v):
    v = list(v)
    for i, j in _SORT16:
        _cmpx(v, i, j)
    return v


def _merge_top16(a, b):
    c = list(a)
    for k in range(len(b)):
        c[PEER_TOPK - 1 - k] = jnp.maximum(a[PEER_TOPK - 1 - k], b[k])
    stride = PEER_TOPK // 2
    while stride >= 1:
        for i in range(PEER_TOPK):
            if i & stride == 0:
                _cmpx(c, i, i + stride)
        stride //= 2
    return c


def _top16_desc(vals):
    runs = [_sort16_desc(vals[i:i + PEER_TOPK]) for i in range(0, len(vals), PEER_TOPK)]
    while len(runs) > 1:
        runs = [_merge_top16(runs[i], runs[i + 1]) for i in range(0, len(runs), 2)]
    return runs[0]


_CAND = [(i, j) for i in range(PEER_TOPK) for j in range(PEER_TOPK) if (i + 1) * (j + 1) <= PEER_TOPK]


def _dup_bf16_words(v):
    bits = lax.bitcast_convert_type(v.astype(_BF16).astype(_F32), jnp.uint32)
    return bits | (bits >> 16)


def _retrieve_kernel(ht_ref, wq_ref, keys_ref, cnt1_ref, e1_ref, rho2_ref, e2_ref,
                     s_buf, x_buf, *, tn, chunk):
    q_t = _dot(wq_ref[...], ht_ref[...]).astype(_BF16)
    half_rows = PEER_HEADS * PEER_HALF
    for p in range(2):
        s_buf[p] = _dot(keys_ref[p], q_t[p * half_rows:(p + 1) * half_rows, :])

    neg = jnp.full((PEER_HEADS, chunk), -jnp.inf, _F32)

    def slab(k):
        return pl.ds(pl.multiple_of(k * PEER_HEADS, PEER_HEADS), PEER_HEADS)

    def chunk_body(c, _):
        ls = pl.ds(pl.multiple_of(c * chunk, chunk), chunk)
        a = _top16_desc([s_buf[0, k * PEER_HEADS:(k + 1) * PEER_HEADS, ls] for k in range(PEER_NKEYS)])
        b = _top16_desc([s_buf[1, k * PEER_HEADS:(k + 1) * PEER_HEADS, ls] for k in range(PEER_NKEYS)])
        rows = {}
        for i, j in _CAND:
            rows.setdefault(i, []).append(a[i] + b[j])
        top = rows[0]
        rest = [v for i in range(1, PEER_TOPK) for v in rows[i]]
        while rest:
            grp, rest = rest[:PEER_TOPK], rest[PEER_TOPK:]
            grp = grp + [neg] * (PEER_TOPK - len(grp))
            top = _merge_top16(top, _sort16_desc(grp))
        c16 = top[PEER_TOPK - 1]
        ea = [jnp.exp(a[i] - a[0]) for i in range(PEER_TOPK)]
        eb = [jnp.exp(b[j] - b[0]) for j in range(PEER_TOPK)]
        z = jnp.zeros((PEER_HEADS, chunk), _F32)
        for i, j in _CAND:
            z = z + jnp.where(a[i] + b[j] >= c16, ea[i] * eb[j], 0.0)
        zinv = 1.0 / z

        def key_body(kk, _):
            for u in range(KEY_UNROLL):
                k = kk * KEY_UNROLL + u
                s1 = s_buf[0, slab(k), ls]
                s2 = s_buf[1, slab(k), ls]
                cnt = jnp.zeros((PEER_HEADS, chunk), _F32)
                rho = jnp.zeros((PEER_HEADS, chunk), _F32)
                for j in range(PEER_TOPK):
                    cnt = jnp.where(s1 + b[j] >= c16, j + 1.0, cnt)
                    rho = jnp.where(b[j] > s2, j + 1.0, rho)
                x_buf[0, slab(k), :] = lax.bitcast_convert_type(_dup_bf16_words(cnt), _F32)
                x_buf[1, slab(k), :] = lax.bitcast_convert_type(
                    _dup_bf16_words(jnp.exp(s1 - a[0]) * zinv), _F32)
                x_buf[2, slab(k), :] = rho
                x_buf[3, slab(k), :] = jnp.exp(s2 - b[0])
            return 0

        lax.fori_loop(0, PEER_NKEYS // KEY_UNROLL, key_body, 0)

        for h in range(PEER_HEADS):
            rows_h = pl.ds(h, PEER_NKEYS, stride=PEER_HEADS)
            cnt1_ref[h, :, ls] = lax.bitcast_convert_type(x_buf[0, rows_h, :], jnp.uint32)
            e1_ref[h, :, ls] = lax.bitcast_convert_type(x_buf[1, rows_h, :], jnp.uint32)
            rho2_ref[h, :, ls] = x_buf[2, rows_h, :].astype(_BF16)
            e2_ref[h, :, ls] = x_buf[3, rows_h, :].astype(_BF16)
        return 0

    lax.fori_loop(0, tn // chunk, chunk_body, 0)


def _retrieve(h_t, wq_t, keys_kh, *, tn, chunk=LANES):
    N = h_t.shape[1]
    assert N % tn == 0 and tn % chunk == 0 and chunk == LANES
    sds = lambda dt: jax.ShapeDtypeStruct((PEER_HEADS, PEER_NKEYS, N), dt)
    out_spec = pl.BlockSpec((PEER_HEADS, PEER_NKEYS, tn), lambda i: (0, 0, i))
    return pl.pallas_call(
        functools.partial(_retrieve_kernel, tn=tn, chunk=chunk),
        grid=(N // tn,),
        in_specs=[
            pl.BlockSpec((D_MODEL, tn), lambda i: (0, i)),
            pl.BlockSpec(wq_t.shape, lambda i: (0, 0), pipeline_mode=pl.Buffered(1)),
            pl.BlockSpec(keys_kh.shape, lambda i: (0, 0, 0), pipeline_mode=pl.Buffered(1)),
        ],
        out_specs=[out_spec] * 4,
        out_shape=[sds(jnp.uint32), sds(jnp.uint32), sds(_BF16), sds(_BF16)],
        scratch_shapes=[
            pltpu.VMEM((2, PEER_NKEYS * PEER_HEADS, tn), _F32),
            pltpu.VMEM((4, PEER_NKEYS * PEER_HEADS, chunk), _F32),
        ],
        compiler_params=pltpu.CompilerParams(
            dimension_semantics=("arbitrary",), vmem_limit_bytes=VMEM_LIMIT),
        name="peer_retrieve",
    )(h_t, wq_t, keys_kh)


def _experts_kernel(ht_ref, h1_ref, u_ref, vt_ref, cnt1_ref, e1_ref, rho2_ref, e2_ref, g_ref, b_ref,
                    out_ref, z0_buf, z1_buf, gz0_buf, gz1_buf, acc_buf, *, tn, te, n_e, lane_chunk):
    j = pl.program_id(0)
    rows = te // PEER_NKEYS
    e_prev = (j + n_e - 1) % n_e
    e_pp = (j + n_e - 2) % n_e

    @pl.when(j == 0)
    def _():
        z1_buf[...] = jnp.zeros_like(z1_buf)
        gz0_buf[...] = jnp.zeros_like(gz0_buf)
        gz1_buf[...] = jnp.zeros_like(gz1_buf)

    @pl.when(jnp.logical_or(e_pp == 0, j == 0))
    def _():
        acc_buf[...] = jnp.zeros_like(acc_buf)

    def stages(z_new, z_cur, gz_cur, gz_old):
        k1_base = pl.multiple_of(e_prev * rows, rows)
        zero = jnp.zeros((PACK, lane_chunk), _BF16)

        def weights(r_lo, r_hi):
            for c in range(tn // lane_chunk):
                ls = slice(c * lane_chunk, (c + 1) * lane_chunk)
                cnt1 = [cnt1_ref[h, pl.ds(k1_base, rows), ls] for h in range(PEER_HEADS)]
                e1 = [e1_ref[h, pl.ds(k1_base, rows), ls] for h in range(PEER_HEADS)]
                for r in range(r_lo, r_hi):
                    def bcast(w):
                        w8 = jnp.broadcast_to(w[r:r + 1, :], (SUBLANES, lane_chunk))
                        return pltpu.bitcast(w8, _BF16)
                    n_q = PEER_NKEYS // PACK
                    g = [None] * n_q
                    for h in range(PEER_HEADS):
                        cb = bcast(cnt1[h])
                        eb = bcast(e1[h])
                        for q in range(n_q):
                            ks = slice(q * PACK, (q + 1) * PACK)
                            m = rho2_ref[h, ks, ls] < cb
                            term = jnp.where(m, e2_ref[h, ks, ls], zero) * eb
                            g[q] = term if g[q] is None else g[q] + term
                    for q in range(n_q):
                        rs = slice(r * PEER_NKEYS + q * PACK, r * PEER_NKEYS + (q + 1) * PACK)
                        zz = z_cur[rs, ls].astype(_BF16)
                        gz_cur[rs, ls] = g[q] * _gelu_sig(zz)

        weights(0, rows)
        z_new[...] = _dot_f32_lhs(u_ref[...], ht_ref[...])
        acc_buf[...] += _dot_f32_lhs(vt_ref[...], gz_old[...])

    @pl.when(j % 2 == 0)
    def _():
        stages(z0_buf, z1_buf, gz1_buf, gz0_buf)

    @pl.when(j % 2 == 1)
    def _():
        stages(z1_buf, z0_buf, gz0_buf, gz1_buf)

    @pl.when(jnp.logical_and(e_pp == n_e - 1, j >= 2))
    def _():
        ffn = acc_buf[...].T
        out_ref[...] = _layer_norm(ALPHA * h1_ref[...] + ffn, g_ref[...], b_ref[...])


def _experts(h_t, h1, u_b, vt_b, cnt1, e1, rho2, e2, ln_g, ln_b, *, tn, te, lane_chunk=256):
    N = h1.shape[0]
    assert N % tn == 0 and N_EXPERTS % te == 0 and te % (SUBLANES * PEER_NKEYS) == 0
    n_e = N_EXPERTS // te
    total = (N // tn) * n_e
    last = total - 1
    tile0 = lambda j: jnp.minimum(j, last)
    tile1 = lambda j: jnp.clip(j - 1, 0, last)
    tile2 = lambda j: jnp.clip(j - 2, 0, last)
    sel_spec = pl.BlockSpec((PEER_HEADS, PEER_NKEYS, tn), lambda j: (0, 0, tile1(j) // n_e))
    return pl.pallas_call(
        functools.partial(_experts_kernel, tn=tn, te=te, n_e=n_e, lane_chunk=min(lane_chunk, tn)),
        grid=(total + 2,),
        in_specs=[
            pl.BlockSpec((D_MODEL, tn), lambda j: (0, tile0(j) // n_e)),
            pl.BlockSpec((tn, D_MODEL), lambda j: (tile2(j) // n_e, 0)),
            pl.BlockSpec((te, D_MODEL), lambda j: (tile0(j) % n_e, 0)),
            pl.BlockSpec((D_MODEL, te), lambda j: (0, tile2(j) % n_e)),
            sel_spec, sel_spec, sel_spec, sel_spec,
            pl.BlockSpec((1, D_MODEL), lambda j: (0, 0)),
            pl.BlockSpec((1, D_MODEL), lambda j: (0, 0)),
        ],
        out_specs=pl.BlockSpec((tn, D_MODEL), lambda j: (tile2(j) // n_e, 0)),
        out_shape=jax.ShapeDtypeStruct((N, D_MODEL), _F32),
        scratch_shapes=[
            pltpu.VMEM((te, tn), _F32),
            pltpu.VMEM((te, tn), _F32),
            pltpu.VMEM((te, tn), _BF16),
            pltpu.VMEM((te, tn), _BF16),
            pltpu.VMEM((D_MODEL, tn), _F32),
        ],
        compiler_params=pltpu.CompilerParams(
            dimension_semantics=("arbitrary",), vmem_limit_bytes=VMEM_LIMIT),
        name="peer_experts",
    )(h_t, h1, u_b, vt_b, cnt1, e1, rho2, e2, ln_g, ln_b)


def _pick_tile(n, target):
    t = min(n, target)
    while n % t:
        t //= 2
    return t


def kernel(x, meta, ln_in_g, ln_in_b, w_in, conv_w, conv_b, w_rg, b_rg, w_ig, b_ig, lru_L, w_proj_a, pool_w, pool_scale, w_proj_b, w_out, ln1_g, ln1_b, w_q, sub_keys, expert_u, expert_v, ln2_g, ln2_b):
    B, S, D = x.shape
    assert D == D_MODEL and w_in.shape[0] == DEPTH
    row = lambda v: v.reshape(1, -1).astype(_F32)
    l = 0
    mixer_weights = (
        row(ln_in_g), row(ln_in_b), w_in[l].astype(_BF16), conv_w[l].astype(_F32), row(conv_b[l]),
        jnp.concatenate([w_rg[l], w_ig[l]], axis=-1).astype(_BF16), row(b_rg[l]), row(b_ig[l]),
        row(lru_L[l]), w_proj_a[l].astype(_BF16), pool_w[l].astype(_BF16), row(pool_scale[l]),
        w_proj_b[l].astype(_BF16), w_out[l].astype(_BF16), row(ln1_g[l]), row(ln1_b[l]),
    )
    zero_state = (jnp.zeros((CONV_TAIL, D_RNN), _F32), jnp.zeros((POOL_TAIL, D_POOL), _F32),
                  jnp.zeros((SUBLANES, D_RNN), _F32))
    _, xr_t, xp_t, hs = _mixer(meta[None].astype(_F32), mixer_weights, zero_state,
                               tt=N_META, pos0=0, emit_state=True)
    h1, h_t = _mixer(x, mixer_weights, (xr_t, xp_t, hs), tt=_pick_tile(S, 256), pos0=N_META,
                     emit_state=False)
    h1 = h1.reshape(B * S, D)
    N = B * S

    wq_t = w_q[l].reshape(D, PEER_HEADS, 2, PEER_HALF).transpose(2, 1, 3, 0)
    wq_t = wq_t.reshape(2 * PEER_HEADS * PEER_HALF, D).astype(_BF16)
    eye = jnp.eye(PEER_HEADS, dtype=_F32)
    keys_kh = jnp.einsum("hpkd,hg->pkhgd", sub_keys[l], eye)
    keys_kh = keys_kh.reshape(2, PEER_NKEYS * PEER_HEADS, PEER_HEADS * PEER_HALF).astype(_BF16)
    tn = _pick_tile(N, 512)
    cnt1, e1, rho2, e2 = _retrieve(h_t, wq_t, keys_kh, tn=tn)

    u_b = expert_u[l]
    vt_b = expert_v[l].T
    out = _experts(h_t, h1, u_b, vt_b, cnt1, e1, rho2, e2, row(ln2_g[l]), row(ln2_b[l]),
                   tn=tn, te=1024)
    return out.reshape(B, S, D).astype(x.dtype)
```

```python
import functools

import jax
import jax.numpy as jnp
from jax import lax
from jax.experimental import pallas as pl
from jax.experimental.pallas import tpu as pltpu

D_MODEL = 1024
N_META = 16
D_RNN = D_MODEL
N_RNN_BLOCKS = 4
RNN_BLOCK = D_RNN // N_RNN_BLOCKS
CONV_WIDTH = 4
LRU_C = 8.0
POOL_WINDOWS = (2, 4, 8, 16)
D_POOL = D_MODEL // 2
POOL_GROUP = D_POOL // len(POOL_WINDOWS)
PEER_HEADS = 8
PEER_NKEYS = 128
N_EXPERTS = PEER_NKEYS * PEER_NKEYS
PEER_HALF = 128
PEER_TOPK = 16
DEPTH = 1
ALPHA = (2.0 * DEPTH) ** 0.25
LN_EPS = 1e-5

SUBLANES = 8
PACK = 16
KEY_UNROLL = 4
LANES = 128
CONV_TAIL = SUBLANES
POOL_TAIL = 16
VMEM_LIMIT = 56 * 1024 * 1024

_F32 = jnp.float32
_BF16 = jnp.bfloat16


def _layer_norm(v, g, b):
    mu = jnp.mean(v, axis=-1, keepdims=True)
    vc = v - mu
    var = jnp.mean(vc * vc, axis=-1, keepdims=True)
    return vc * lax.rsqrt(var + LN_EPS) * g + b


def _gelu_sig(v):
    w = v * (-2.3022082870680315 - 0.10294324120074478 * (v * v))
    return v / (1.0 + jnp.exp2(w))


def _sigmoid(v):
    return 1.0 / (1.0 + jnp.exp(-v))


def _dot(a, b):
    return jnp.dot(a, b, preferred_element_type=_F32)


def _dot_f32_lhs(a, b):
    return lax.dot_general(a, b, (((1,), (0,)), ((), ())), preferred_element_type=_F32)


def _lru_scan(a, b, h_prev):
    tt, c = a.shape
    groups = tt // SUBLANES
    a3 = a.reshape(groups, SUBLANES, c)
    b3 = b.reshape(groups, SUBLANES, c)
    row = lax.broadcasted_iota(jnp.int32, a3.shape, 1)
    shift = 1
    while shift < SUBLANES:
        a_sh = pltpu.roll(a3, shift, axis=1)
        b_sh = pltpu.roll(b3, shift, axis=1)
        m = row >= shift
        b3 = jnp.where(m, a3 * b_sh + b3, b3)
        a3 = jnp.where(m, a3 * a_sh, a3)
        shift *= 2
    outs = []
    carry = h_prev
    for g in range(groups):
        hg = a3[g] * carry + b3[g]
        outs.append(hg)
        carry = hg[SUBLANES - 1:SUBLANES, :]
    return jnp.concatenate(outs, axis=0)


def _mixer_kernel(x_ref, lng_ref, lnb_ref, win_ref, convw_ref, convb_ref, wgate_ref, brg_ref,
                  big_ref, lrul_ref, wpa_ref, poolw_ref, pscale_ref, wpb_ref, wout_ref,
                  ln1g_ref, ln1b_ref, xr0_ref, xp0_ref, hs0_ref,
                  h1_ref, *rest, tt, pos0, emit_state):
    if emit_state:
        xr_out, xp_out, hs_out, xr_buf, xp_buf, hs_buf, rg_buf, mx_buf = rest
    else:
        ht_ref, xr_buf, xp_buf, hs_buf, rg_buf, mx_buf = rest
    t = pl.program_id(1)

    @pl.when(t == 0)
    def _():
        xr_buf[0:CONV_TAIL, :] = xr0_ref[...]
        xp_buf[0:POOL_TAIL, :] = xp0_ref[...]
        hs_buf[...] = hs0_ref[...]

    x = x_ref[0]
    h0 = _layer_norm(x, lng_ref[...], lnb_ref[...])

    xr_buf[CONV_TAIL:CONV_TAIL + tt, :] = _dot_f32_lhs(h0, win_ref[:, 0:D_RNN])
    lsig = lrul_ref[...]
    log_sig = jnp.minimum(lsig, 0.0) - jnp.log1p(jnp.exp(-jnp.abs(lsig)))
    for n in range(N_RNN_BLOCKS):
        cs = slice(n * RNN_BLOCK, (n + 1) * RNN_BLOCK)
        xc = convb_ref[:, cs]
        for j in range(CONV_WIDTH):
            off = CONV_TAIL - (CONV_WIDTH - 1) + j
            xc = xc + convw_ref[j:j + 1, cs] * xr_buf[off:off + tt, cs]
        pre = _dot_f32_lhs(xc, wgate_ref[n])
        r = _sigmoid(pre[:, :RNN_BLOCK] + brg_ref[:, cs])
        i = _sigmoid(pre[:, RNN_BLOCK:] + big_ref[:, cs])
        log_a = (LRU_C * log_sig[:, cs]) * r
        a = jnp.exp(log_a)
        mult = jnp.sqrt(1.0 - a * a)
        h = _lru_scan(a, mult * i * xc, hs_buf[0:1, cs])
        hs_buf[:, cs] = jnp.broadcast_to(h[tt - 1:tt, :], (SUBLANES, RNN_BLOCK))
        gate = _dot_f32_lhs(h0, win_ref[:, D_RNN + n * RNN_BLOCK:D_RNN + (n + 1) * RNN_BLOCK])
        rg_buf[:, cs] = h * _gelu_sig(gate)
    y_a = _dot_f32_lhs(rg_buf[...], wpa_ref[...])

    xp_buf[POOL_TAIL:POOL_TAIL + tt, :] = _dot_f32_lhs(h0, win_ref[:, 2 * D_RNN:2 * D_RNN + D_POOL])
    if pos0 < max(POOL_WINDOWS):
        pos = (lax.broadcasted_iota(jnp.int32, (tt, 1), 0) + (pos0 + 1) + t * tt).astype(_F32)
    for g, w in enumerate(POOL_WINDOWS):
        cs = slice(g * POOL_GROUP, (g + 1) * POOL_GROUP)
        cur = xp_buf[POOL_TAIL:POOL_TAIL + tt, cs]
        acc = cur
        for j in range(1, w):
            acc = acc + xp_buf[POOL_TAIL - j:POOL_TAIL - j + tt, cs]
        if pos0 < max(POOL_WINDOWS):
            mean = acc / jnp.minimum(pos, float(w))
        else:
            mean = acc * (1.0 / w)
        mixed = _dot_f32_lhs(mean - cur, poolw_ref[g]) * pscale_ref[:, cs]
        mx_buf[:, cs] = mixed
    y_b = _dot_f32_lhs(mx_buf[...], wpb_ref[...])

    g_off = 2 * D_RNN + D_POOL
    gate_a = _sigmoid(_dot_f32_lhs(h0, win_ref[:, g_off:g_off + D_MODEL]))
    gate_b = _sigmoid(_dot_f32_lhs(h0, win_ref[:, g_off + D_MODEL:g_off + 2 * D_MODEL]))
    merged = gate_a * y_a + gate_b * y_b
    mix = _dot_f32_lhs(merged, wout_ref[...])
    h1 = _layer_norm(ALPHA * h0 + mix, ln1g_ref[...], ln1b_ref[...])
    h1_ref[0] = h1
    if not emit_state:
        ht_ref[...] = h1.T.astype(_BF16)

    xr_buf[0:CONV_TAIL, :] = xr_buf[tt:tt + CONV_TAIL, :]
    xp_buf[0:POOL_TAIL, :] = xp_buf[tt:tt + POOL_TAIL, :]
    if emit_state:
        xr_out[...] = xr_buf[0:CONV_TAIL, :]
        xp_out[...] = xp_buf[0:POOL_TAIL, :]
        hs_out[...] = hs_buf[...]


def _const_spec(shape):
    nd = len(shape)
    return pl.BlockSpec(shape, lambda b, t: (0,) * nd, pipeline_mode=pl.Buffered(1))


def _mixer(x, weights, state, *, tt, pos0, emit_state):
    B, S, D = x.shape
    assert S % tt == 0 and tt % POOL_TAIL == 0
    in_specs = [pl.BlockSpec((1, tt, D), lambda b, t: (b, t, 0))]
    in_specs += [_const_spec(w.shape) for w in weights]
    in_specs += [_const_spec(s.shape) for s in state]
    out_shape = [jax.ShapeDtypeStruct((B, S, D), _F32)]
    out_specs = [pl.BlockSpec((1, tt, D), lambda b, t: (b, t, 0))]
    if emit_state:
        assert B == 1 and S == tt
        out_shape += [jax.ShapeDtypeStruct(s.shape, _F32) for s in state]
        out_specs += [pl.BlockSpec(s.shape, lambda b, t: (0, 0)) for s in state]
    else:
        assert tt % LANES == 0
        n_t = S // tt
        out_shape += [jax.ShapeDtypeStruct((D, B * S), _BF16)]
        out_specs += [pl.BlockSpec((D, tt), lambda b, t: (0, b * n_t + t))]
    scratch = [
        pltpu.VMEM((CONV_TAIL + tt, D_RNN), _F32),
        pltpu.VMEM((POOL_TAIL + tt, D_POOL), _F32),
        pltpu.VMEM((SUBLANES, D_RNN), _F32),
        pltpu.VMEM((tt, D_RNN), _F32),
        pltpu.VMEM((tt, D_POOL), _F32),
    ]
    return pl.pallas_call(
        functools.partial(_mixer_kernel, tt=tt, pos0=pos0, emit_state=emit_state),
        grid=(B, S // tt),
        in_specs=in_specs,
        out_specs=out_specs,
        out_shape=out_shape,
        scratch_shapes=scratch,
        compiler_params=pltpu.CompilerParams(
            dimension_semantics=("arbitrary", "arbitrary"), vmem_limit_bytes=VMEM_LIMIT),
        name="mixer_meta" if emit_state else "mixer",
    )(x, *weights, *state)


def _oddeven_merge_sort_pairs(n):
    pairs = []

    def merge(lo, hi, r):
        step = r * 2
        if step < hi - lo:
            merge(lo, hi, step)
            merge(lo + r, hi, step)
            for i in range(lo + r, hi - r, step):
                pairs.append((i, i + r))
        else:
            pairs.append((lo, lo + r))

    def sort(lo, hi):
        if hi - lo >= 1:
            mid = lo + (hi - lo) // 2
            sort(lo, mid)
            sort(mid + 1, hi)
            merge(lo, hi, 1)

    sort(0, n - 1)
    return pairs


_SORT16 = _oddeven_merge_sort_pairs(PEER_TOPK)


def _cmpx(v, i, j):
    hi = jnp.maximum(v[i], v[j])
    lo = jnp.minimum(v[i], v[j])
    v[i] = hi
    v[j] = lo


def _sort16_desc(v):
    v = list(v)
    for i, j in _SORT16:
        _cmpx(v, i, j)
    return v


def _merge_top16(a, b):
    c = list(a)
    for k in range(len(b)):
        c[PEER_TOPK - 1 - k] = jnp.maximum(a[PEER_TOPK - 1 - k], b[k])
    stride = PEER_TOPK // 2
    while stride >= 1:
        for i in range(PEER_TOPK):
            if i & stride == 0:
                _cmpx(c, i, i + stride)
        stride //= 2
    return c


def _top16_desc(vals):
    runs = [_sort16_desc(vals[i:i + PEER_TOPK]) for i in range(0, len(vals), PEER_TOPK)]
    while len(runs) > 1:
        runs = [_merge_top16(runs[i], runs[i + 1]) for i in range(0, len(runs), 2)]
    return runs[0]


_CAND = [(i, j) for i in range(PEER_TOPK) for j in range(PEER_TOPK) if (i + 1) * (j + 1) <= PEER_TOPK]


def _prefix_count(vals, pred):
    sel = jnp.where
    m8 = pred(vals[7])
    m4 = pred(sel(m8, vals[11], vals[3]))
    m2 = pred(sel(m8, sel(m4, vals[13], vals[9]), sel(m4, vals[5], vals[1])))
    lo = sel(m4, sel(m2, vals[6], vals[4]), sel(m2, vals[2], vals[0]))
    hi = sel(m4, sel(m2, vals[14], vals[12]), sel(m2, vals[10], vals[8]))
    m1 = pred(sel(m8, hi, lo))
    cnt = sel(m8, 8.0, 0.0) + sel(m4, 4.0, 0.0) + sel(m2, 2.0, 0.0) + sel(m1, 1.0, 0.0)
    return sel(pred(vals[15]), 16.0, cnt)


def _dup_bf16_words(v):
    bits = lax.bitcast_convert_type(v.astype(_BF16).astype(_F32), jnp.uint32)
    return bits | (bits >> 16)


def _retrieve_kernel(ht_ref, wq_ref, keys_ref, cnt1_ref, e1_ref, rho2_ref, e2_ref,
                     s_buf, x_buf, *, tn, chunk):
    q_t = _dot(wq_ref[...], ht_ref[...]).astype(_BF16)
    half_rows = PEER_HEADS * PEER_HALF
    for p in range(2):
        s_buf[p] = _dot(keys_ref[p], q_t[p * half_rows:(p + 1) * half_rows, :])

    neg = jnp.full((PEER_HEADS, chunk), -jnp.inf, _F32)

    def slab(k):
        return pl.ds(pl.multiple_of(k * PEER_HEADS, PEER_HEADS), PEER_HEADS)

    def chunk_body(c, _):
        ls = pl.ds(pl.multiple_of(c * chunk, chunk), chunk)
        a = _top16_desc([s_buf[0, k * PEER_HEADS:(k + 1) * PEER_HEADS, ls] for k in range(PEER_NKEYS)])
        b = _top16_desc([s_buf[1, k * PEER_HEADS:(k + 1) * PEER_HEADS, ls] for k in range(PEER_NKEYS)])
        rows = {}
        for i, j in _CAND:
            rows.setdefault(i, []).append(a[i] + b[j])
        top = rows[0]
        rest = [v for i in range(1, PEER_TOPK) for v in rows[i]]
        while rest:
            grp, rest = rest[:PEER_TOPK], rest[PEER_TOPK:]
            grp = grp + [neg] * (PEER_TOPK - len(grp))
            top = _merge_top16(top, _sort16_desc(grp))
        c16 = top[PEER_TOPK - 1]
        ea = [jnp.exp(a[i] - a[0]) for i in range(PEER_TOPK)]
        eb = [jnp.exp(b[j] - b[0]) for j in range(PEER_TOPK)]
        z = jnp.zeros((PEER_HEADS, chunk), _F32)
        for i, j in _CAND:
            z = z + jnp.where(a[i] + b[j] >= c16, ea[i] * eb[j], 0.0)
        zinv = 1.0 / z

        def key_body(kk, _):
            for u in range(KEY_UNROLL):
                k = kk * KEY_UNROLL + u
                s1 = s_buf[0, slab(k), ls]
                s2 = s_buf[1, slab(k), ls]
                cnt = _prefix_count(b, lambda v: s1 + v >= c16)
                rho = _prefix_count(b, lambda v: v > s2)
                x_buf[0, slab(k), :] = lax.bitcast_convert_type(_dup_bf16_words(cnt), _F32)
                x_buf[1, slab(k), :] = lax.bitcast_convert_type(
                    _dup_bf16_words(jnp.exp(s1 - a[0]) * zinv), _F32)
                x_buf[2, slab(k), :] = rho
                x_buf[3, slab(k), :] = jnp.exp(s2 - b[0])
            return 0

        lax.fori_loop(0, PEER_NKEYS // KEY_UNROLL, key_body, 0)

        for h in range(PEER_HEADS):
            rows_h = pl.ds(h, PEER_NKEYS, stride=PEER_HEADS)
            cnt1_ref[h, :, ls] = lax.bitcast_convert_type(x_buf[0, rows_h, :], jnp.uint32)
            e1_ref[h, :, ls] = lax.bitcast_convert_type(x_buf[1, rows_h, :], jnp.uint32)
            rho2_ref[h, :, ls] = x_buf[2, rows_h, :].astype(_BF16)
            e2_ref[h, :, ls] = x_buf[3, rows_h, :].astype(_BF16)
        return 0

    lax.fori_loop(0, tn // chunk, chunk_body, 0)


def _retrieve(h_t, wq_t, keys_kh, *, tn, chunk=LANES):
    N = h_t.shape[1]
    assert N % tn == 0 and tn % chunk == 0 and chunk == LANES
    sds = lambda dt: jax.ShapeDtypeStruct((PEER_HEADS, PEER_NKEYS, N), dt)
    out_spec = pl.BlockSpec((PEER_HEADS, PEER_NKEYS, tn), lambda i: (0, 0, i))
    return pl.pallas_call(
        functools.partial(_retrieve_kernel, tn=tn, chunk=chunk),
        grid=(N // tn,),
        in_specs=[
            pl.BlockSpec((D_MODEL, tn), lambda i: (0, i)),
            pl.BlockSpec(wq_t.shape, lambda i: (0, 0), pipeline_mode=pl.Buffered(1)),
            pl.BlockSpec(keys_kh.shape, lambda i: (0, 0, 0), pipeline_mode=pl.Buffered(1)),
        ],
        out_specs=[out_spec] * 4,
        out_shape=[sds(jnp.uint32), sds(jnp.uint32), sds(_BF16), sds(_BF16)],
        scratch_shapes=[
            pltpu.VMEM((2, PEER_NKEYS * PEER_HEADS, tn), _F32),
            pltpu.VMEM((4, PEER_NKEYS * PEER_HEADS, chunk), _F32),
        ],
        compiler_params=pltpu.CompilerParams(
            dimension_semantics=("arbitrary",), vmem_limit_bytes=VMEM_LIMIT),
        name="peer_retrieve",
    )(h_t, wq_t, keys_kh)


def _experts_kernel(ht_ref, h1_ref, u_ref, vt_ref, cnt1_ref, e1_ref, rho2_ref, e2_ref, g_ref, b_ref,
                    out_ref, z0_buf, z1_buf, gz0_buf, gz1_buf, acc_buf, *, tn, te, n_e, lane_chunk):
    j = pl.program_id(0)
    rows = te // PEER_NKEYS
    e_prev = (j + n_e - 1) % n_e
    e_pp = (j + n_e - 2) % n_e

    @pl.when(j == 0)
    def _():
        z1_buf[...] = jnp.zeros_like(z1_buf)
        gz0_buf[...] = jnp.zeros_like(gz0_buf)
        gz1_buf[...] = jnp.zeros_like(gz1_buf)

    @pl.when(jnp.logical_or(e_pp == 0, j == 0))
    def _():
        acc_buf[...] = jnp.zeros_like(acc_buf)

    def stages(z_new, z_cur, gz_cur, gz_old):
        k1_base = pl.multiple_of(e_prev * rows, rows)
        zero = jnp.zeros((PACK, lane_chunk), _BF16)

        def weights(r_lo, r_hi):
            for c in range(tn // lane_chunk):
                ls = slice(c * lane_chunk, (c + 1) * lane_chunk)
                cnt1 = [cnt1_ref[h, pl.ds(k1_base, rows), ls] for h in range(PEER_HEADS)]
                e1 = [e1_ref[h, pl.ds(k1_base, rows), ls] for h in range(PEER_HEADS)]
                for r in range(r_lo, r_hi):
                    def bcast(w):
                        w8 = jnp.broadcast_to(w[r:r + 1, :], (SUBLANES, lane_chunk))
                        return pltpu.bitcast(w8, _BF16)
                    n_q = PEER_NKEYS // PACK
                    g = [None] * n_q
                    for h in range(PEER_HEADS):
                        cb = bcast(cnt1[h])
                        eb = bcast(e1[h])
                        for q in range(n_q):
                            ks = slice(q * PACK, (q + 1) * PACK)
                            m = rho2_ref[h, ks, ls] < cb
                            term = jnp.where(m, e2_ref[h, ks, ls], zero) * eb
                            g[q] = term if g[q] is None else g[q] + term
                    for q in range(n_q):
                        rs = slice(r * PEER_NKEYS + q * PACK, r * PEER_NKEYS + (q + 1) * PACK)
                        zz = z_cur[rs, ls].astype(_BF16)
                        gz_cur[rs, ls] = g[q] * _gelu_sig(zz)

        weights(0, rows)
        z_new[...] = _dot_f32_lhs(u_ref[...], ht_ref[...])
        acc_buf[...] += _dot_f32_lhs(vt_ref[...], gz_old[...])

    @pl.when(j % 2 == 0)
    def _():
        stages(z0_buf, z1_buf, gz1_buf, gz0_buf)

    @pl.when(j % 2 == 1)
    def _():
        stages(z1_buf, z0_buf, gz0_buf, gz1_buf)

    @pl.when(jnp.logical_and(e_pp == n_e - 1, j >= 2))
    def _():
        ffn = acc_buf[...].T
        out_ref[...] = _layer_norm(ALPHA * h1_ref[...] + ffn, g_ref[...], b_ref[...])


def _experts(h_t, h1, u_b, vt_b, cnt1, e1, rho2, e2, ln_g, ln_b, *, tn, te, lane_chunk=256):
    N = h1.shape[0]
    assert N % tn == 0 and N_EXPERTS % te == 0 and te % (SUBLANES * PEER_NKEYS) == 0
    n_e = N_EXPERTS // te
    total = (N // tn) * n_e
    last = total - 1
    tile0 = lambda j: jnp.minimum(j, last)
    tile1 = lambda j: jnp.clip(j - 1, 0, last)
    tile2 = lambda j: jnp.clip(j - 2, 0, last)
    sel_spec = pl.BlockSpec((PEER_HEADS, PEER_NKEYS, tn), lambda j: (0, 0, tile1(j) // n_e))
    return pl.pallas_call(
        functools.partial(_experts_kernel, tn=tn, te=te, n_e=n_e, lane_chunk=min(lane_chunk, tn)),
        grid=(total + 2,),
        in_specs=[
            pl.BlockSpec((D_MODEL, tn), lambda j: (0, tile0(j) // n_e)),
            pl.BlockSpec((tn, D_MODEL), lambda j: (tile2(j) // n_e, 0)),
            pl.BlockSpec((te, D_MODEL), lambda j: (tile0(j) % n_e, 0)),
            pl.BlockSpec((D_MODEL, te), lambda j: (0, tile2(j) % n_e)),
            sel_spec, sel_spec, sel_spec, sel_spec,
            pl.BlockSpec((1, D_MODEL), lambda j: (0, 0)),
            pl.BlockSpec((1, D_MODEL), lambda j: (0, 0)),
        ],
        out_specs=pl.BlockSpec((tn, D_MODEL), lambda j: (tile2(j) // n_e, 0)),
        out_shape=jax.ShapeDtypeStruct((N, D_MODEL), _F32),
        scratch_shapes=[
            pltpu.VMEM((te, tn), _F32),
            pltpu.VMEM((te, tn), _F32),
            pltpu.VMEM((te, tn), _BF16),
            pltpu.VMEM((te, tn), _BF16),
            pltpu.VMEM((D_MODEL, tn), _F32),
        ],
        compiler_params=pltpu.CompilerParams(
            dimension_semantics=("arbitrary",), vmem_limit_bytes=VMEM_LIMIT),
        name="peer_experts",
    )(h_t, h1, u_b, vt_b, cnt1, e1, rho2, e2, ln_g, ln_b)


def _pick_tile(n, target):
    t = min(n, target)
    while n % t:
        t //= 2
    return t


def kernel(x, meta, ln_in_g, ln_in_b, w_in, conv_w, conv_b, w_rg, b_rg, w_ig, b_ig, lru_L, w_proj_a, pool_w, pool_scale, w_proj_b, w_out, ln1_g, ln1_b, w_q, sub_keys, expert_u, expert_v, ln2_g, ln2_b):
    B, S, D = x.shape
    assert D == D_MODEL and w_in.shape[0] == DEPTH
    row = lambda v: v.reshape(1, -1).astype(_F32)
    l = 0
    mixer_weights = (
        row(ln_in_g), row(ln_in_b), w_in[l].astype(_BF16), conv_w[l].astype(_F32), row(conv_b[l]),
        jnp.concatenate([w_rg[l], w_ig[l]], axis=-1).astype(_BF16), row(b_rg[l]), row(b_ig[l]),
        row(lru_L[l]), w_proj_a[l].astype(_BF16), pool_w[l].astype(_BF16), row(pool_scale[l]),
        w_proj_b[l].astype(_BF16), w_out[l].astype(_BF16), row(ln1_g[l]), row(ln1_b[l]),
    )
    zero_state = (jnp.zeros((CONV_TAIL, D_RNN), _F32), jnp.zeros((POOL_TAIL, D_POOL), _F32),
                  jnp.zeros((SUBLANES, D_RNN), _F32))
    _, xr_t, xp_t, hs = _mixer(meta[None].astype(_F32), mixer_weights, zero_state,
                               tt=N_META, pos0=0, emit_state=True)
    h1, h_t = _mixer(x, mixer_weights, (xr_t, xp_t, hs), tt=_pick_tile(S, 256), pos0=N_META,
                     emit_state=False)
    h1 = h1.reshape(B * S, D)
    N = B * S

    wq_t = w_q[l].reshape(D, PEER_HEADS, 2, PEER_HALF).transpose(2, 1, 3, 0)
    wq_t = wq_t.reshape(2 * PEER_HEADS * PEER_HALF, D).astype(_BF16)
    eye = jnp.eye(PEER_HEADS, dtype=_F32)
    keys_kh = jnp.einsum("hpkd,hg->pkhgd", sub_keys[l], eye)
    keys_kh = keys_kh.reshape(2, PEER_NKEYS * PEER_HEADS, PEER_HEADS * PEER_HALF).astype(_BF16)
    tn = _pick_tile(N, 512)
    cnt1, e1, rho2, e2 = _retrieve(h_t, wq_t, keys_kh, tn=tn)

    u_b = expert_u[l]
    vt_b = expert_v[l].T
    out = _experts(h_t, h1, u_b, vt_b, cnt1, e1, rho2, e2, row(ln2_g[l]), row(ln2_b[l]),
                   tn=tn, te=1024)
    return out.reshape(B, S, D).astype(x.dtype)
```

```python
import functools

import jax
import jax.numpy as jnp
from jax import lax
from jax.experimental import pallas as pl
from jax.experimental.pallas import tpu as pltpu

D_MODEL = 1024
N_META = 16
D_RNN = D_MODEL
N_RNN_BLOCKS = 4
RNN_BLOCK = D_RNN // N_RNN_BLOCKS
CONV_WIDTH = 4
LRU_C = 8.0
POOL_WINDOWS = (2, 4, 8, 16)
D_POOL = D_MODEL // 2
POOL_GROUP = D_POOL // len(POOL_WINDOWS)
PEER_HEADS = 8
PEER_NKEYS = 128
N_EXPERTS = PEER_NKEYS * PEER_NKEYS
PEER_HALF = 128
PEER_TOPK = 16
DEPTH = 1
ALPHA = (2.0 * DEPTH) ** 0.25
LN_EPS = 1e-5

SUBLANES = 8
PACK = 16
KEY_UNROLL = 4
LANES = 128
CONV_TAIL = SUBLANES
POOL_TAIL = 16
VMEM_LIMIT = 56 * 1024 * 1024

_F32 = jnp.float32
_BF16 = jnp.bfloat16


def _layer_norm(v, g, b):
    mu = jnp.mean(v, axis=-1, keepdims=True)
    vc = v - mu
    var = jnp.mean(vc * vc, axis=-1, keepdims=True)
    return vc * lax.rsqrt(var + LN_EPS) * g + b


def _gelu_sig(v):
    w = v * (-2.3022082870680315 - 0.10294324120074478 * (v * v))
    return v / (1.0 + jnp.exp2(w))


def _sigmoid(v):
    return 1.0 / (1.0 + jnp.exp(-v))


def _dot_f32_lhs(a, b):
    return lax.dot_general(a, b, (((1,), (0,)), ((), ())), preferred_element_type=_F32)


def _lru_scan(a, b, h_prev):
    tt, c = a.shape
    groups = tt // SUBLANES
    a3 = a.reshape(groups, SUBLANES, c)
    b3 = b.reshape(groups, SUBLANES, c)
    row = lax.broadcasted_iota(jnp.int32, a3.shape, 1)
    shift = 1
    while shift < SUBLANES:
        a_sh = pltpu.roll(a3, shift, axis=1)
        b_sh = pltpu.roll(b3, shift, axis=1)
        m = row >= shift
        b3 = jnp.where(m, a3 * b_sh + b3, b3)
        a3 = jnp.where(m, a3 * a_sh, a3)
        shift *= 2
    outs = []
    carry = h_prev
    for g in range(groups):
        hg = a3[g] * carry + b3[g]
        outs.append(hg)
        carry = hg[SUBLANES - 1:SUBLANES, :]
    return jnp.concatenate(outs, axis=0)


def _mixer_kernel(x_ref, lng_ref, lnb_ref, win_ref, convw_ref, convb_ref, wgate_ref, brg_ref,
                  big_ref, lrul_ref, wpa_ref, poolw_ref, pscale_ref, wpb_ref, wout_ref,
                  ln1g_ref, ln1b_ref, xr0_ref, xp0_ref, hs0_ref,
                  h1_ref, *rest, tt, pos0, emit_state):
    if emit_state:
        xr_out, xp_out, hs_out, xr_buf, xp_buf, hs_buf, rg_buf, mx_buf = rest
    else:
        ht_ref, xr_buf, xp_buf, hs_buf, rg_buf, mx_buf = rest
    t = pl.program_id(1)

    @pl.when(t == 0)
    def _():
        xr_buf[0:CONV_TAIL, :] = xr0_ref[...]
        xp_buf[0:POOL_TAIL, :] = xp0_ref[...]
        hs_buf[...] = hs0_ref[...]

    x = x_ref[0]
    h0 = _layer_norm(x, lng_ref[...], lnb_ref[...])

    xr_buf[CONV_TAIL:CONV_TAIL + tt, :] = _dot_f32_lhs(h0, win_ref[:, 0:D_RNN])
    lsig = lrul_ref[...]
    log_sig = jnp.minimum(lsig, 0.0) - jnp.log1p(jnp.exp(-jnp.abs(lsig)))
    for n in range(N_RNN_BLOCKS):
        cs = slice(n * RNN_BLOCK, (n + 1) * RNN_BLOCK)
        xc = convb_ref[:, cs]
        for j in range(CONV_WIDTH):
            off = CONV_TAIL - (CONV_WIDTH - 1) + j
            xc = xc + convw_ref[j:j + 1, cs] * xr_buf[off:off + tt, cs]
        pre = _dot_f32_lhs(xc, wgate_ref[n])
        r = _sigmoid(pre[:, :RNN_BLOCK] + brg_ref[:, cs])
        i = _sigmoid(pre[:, RNN_BLOCK:] + big_ref[:, cs])
        log_a = (LRU_C * log_sig[:, cs]) * r
        a = jnp.exp(log_a)
        mult = jnp.sqrt(1.0 - a * a)
        h = _lru_scan(a, mult * i * xc, hs_buf[0:1, cs])
        hs_buf[:, cs] = jnp.broadcast_to(h[tt - 1:tt, :], (SUBLANES, RNN_BLOCK))
        gate = _dot_f32_lhs(h0, win_ref[:, D_RNN + n * RNN_BLOCK:D_RNN + (n + 1) * RNN_BLOCK])
        rg_buf[:, cs] = h * _gelu_sig(gate)
    y_a = _dot_f32_lhs(rg_buf[...], wpa_ref[...])

    xp_buf[POOL_TAIL:POOL_TAIL + tt, :] = _dot_f32_lhs(h0, win_ref[:, 2 * D_RNN:2 * D_RNN + D_POOL])
    if pos0 < max(POOL_WINDOWS):
        pos = (lax.broadcasted_iota(jnp.int32, (tt, 1), 0) + (pos0 + 1) + t * tt).astype(_F32)
    for g, w in enumerate(POOL_WINDOWS):
        cs = slice(g * POOL_GROUP, (g + 1) * POOL_GROUP)
        cur = xp_buf[POOL_TAIL:POOL_TAIL + tt, cs]
        acc = cur
        for j in range(1, w):
            acc = acc + xp_buf[POOL_TAIL - j:POOL_TAIL - j + tt, cs]
        if pos0 < max(POOL_WINDOWS):
            mean = acc / jnp.minimum(pos, float(w))
        else:
            mean = acc * (1.0 / w)
        mixed = _dot_f32_lhs(mean - cur, poolw_ref[g]) * pscale_ref[:, cs]
        mx_buf[:, cs] = mixed
    y_b = _dot_f32_lhs(mx_buf[...], wpb_ref[...])

    g_off = 2 * D_RNN + D_POOL
    gate_a = _sigmoid(_dot_f32_lhs(h0, win_ref[:, g_off:g_off + D_MODEL]))
    gate_b = _sigmoid(_dot_f32_lhs(h0, win_ref[:, g_off + D_MODEL:g_off + 2 * D_MODEL]))
    merged = gate_a * y_a + gate_b * y_b
    mix = _dot_f32_lhs(merged, wout_ref[...])
    h1 = _layer_norm(ALPHA * h0 + mix, ln1g_ref[...], ln1b_ref[...])
    h1_ref[0] = h1
    if not emit_state:
        ht_ref[...] = h1.T.astype(_BF16)

    xr_buf[0:CONV_TAIL, :] = xr_buf[tt:tt + CONV_TAIL, :]
    xp_buf[0:POOL_TAIL, :] = xp_buf[tt:tt + POOL_TAIL, :]
    if emit_state:
        xr_out[...] = xr_buf[0:CONV_TAIL, :]
        xp_out[...] = xp_buf[0:POOL_TAIL, :]
        hs_out[...] = hs_buf[...]


def _const_spec(shape):
    nd = len(shape)
    return pl.BlockSpec(shape, lambda b, t: (0,) * nd, pipeline_mode=pl.Buffered(1))


def _mixer(x, weights, state, *, tt, pos0, emit_state):
    B, S, D = x.shape
    assert S % tt == 0 and tt % POOL_TAIL == 0
    in_specs = [pl.BlockSpec((1, tt, D), lambda b, t: (b, t, 0))]
    in_specs += [_const_spec(w.shape) for w in weights]
    in_specs += [_const_spec(s.shape) for s in state]
    out_shape = [jax.ShapeDtypeStruct((B, S, D), _F32)]
    out_specs = [pl.BlockSpec((1, tt, D), lambda b, t: (b, t, 0))]
    if emit_state:
        assert B == 1 and S == tt
        out_shape += [jax.ShapeDtypeStruct(s.shape, _F32) for s in state]
        out_specs += [pl.BlockSpec(s.shape, lambda b, t: (0, 0)) for s in state]
    else:
        assert tt % LANES == 0
        n_t = S // tt
        out_shape += [jax.ShapeDtypeStruct((D, B * S), _BF16)]
        out_specs += [pl.BlockSpec((D, tt), lambda b, t: (0, b * n_t + t))]
    scratch = [
        pltpu.VMEM((CONV_TAIL + tt, D_RNN), _F32),
        pltpu.VMEM((POOL_TAIL + tt, D_POOL), _F32),
        pltpu.VMEM((SUBLANES, D_RNN), _F32),
        pltpu.VMEM((tt, D_RNN), _F32),
        pltpu.VMEM((tt, D_POOL), _F32),
    ]
    return pl.pallas_call(
        functools.partial(_mixer_kernel, tt=tt, pos0=pos0, emit_state=emit_state),
        grid=(B, S // tt),
        in_specs=in_specs,
        out_specs=out_specs,
        out_shape=out_shape,
        scratch_shapes=scratch,
        compiler_params=pltpu.CompilerParams(
            dimension_semantics=("arbitrary", "arbitrary"), vmem_limit_bytes=VMEM_LIMIT),
        name="mixer_meta" if emit_state else "mixer",
    )(x, *weights, *state)


def _oddeven_merge_sort_pairs(n):
    pairs = []

    def merge(lo, hi, r):
        step = r * 2
        if step < hi - lo:
            merge(lo, hi, step)
            merge(lo + r, hi, step)
            for i in range(lo + r, hi - r, step):
                pairs.append((i, i + r))
        else:
            pairs.append((lo, lo + r))

    def sort(lo, hi):
        if hi - lo >= 1:
            mid = lo + (hi - lo) // 2
            sort(lo, mid)
            sort(mid + 1, hi)
            merge(lo, hi, 1)

    sort(0, n - 1)
    return pairs


_SORT16 = _oddeven_merge_sort_pairs(PEER_TOPK)


def _cmpx(v, i, j):
    hi = jnp.maximum(v[i], v[j])
    lo = jnp.minimum(v[i], v[j])
    v[i] = hi
    v[j] = lo


def _sort16_desc(v):
    v = list(v)
    for i, j in _SORT16:
        _cmpx(v, i, j)
    return v


def _merge_top16(a, b):
    c = list(a)
    for k in range(len(b)):
        c[PEER_TOPK - 1 - k] = jnp.maximum(a[PEER_TOPK - 1 - k], b[k])
    stride = PEER_TOPK // 2
    while stride >= 1:
        for i in range(PEER_TOPK):
            if i & stride == 0:
                _cmpx(c, i, i + stride)
        stride //= 2
    return c


def _top16_desc(vals):
    runs = [_sort16_desc(vals[i:i + PEER_TOPK]) for i in range(0, len(vals), PEER_TOPK)]
    while len(runs) > 1:
        runs = [_merge_top16(runs[i], runs[i + 1]) for i in range(0, len(runs), 2)]
    return runs[0]


_CAND = [(i, j) for i in range(PEER_TOPK) for j in range(PEER_TOPK) if (i + 1) * (j + 1) <= PEER_TOPK]


def _prefix_count(vals, pred):
    sel = jnp.where
    m8 = pred(vals[7])
    m4 = pred(sel(m8, vals[11], vals[3]))
    m2 = pred(sel(m8, sel(m4, vals[13], vals[9]), sel(m4, vals[5], vals[1])))
    lo = sel(m4, sel(m2, vals[6], vals[4]), sel(m2, vals[2], vals[0]))
    hi = sel(m4, sel(m2, vals[14], vals[12]), sel(m2, vals[10], vals[8]))
    m1 = pred(sel(m8, hi, lo))
    cnt = sel(m8, 8.0, 0.0) + sel(m4, 4.0, 0.0) + sel(m2, 2.0, 0.0) + sel(m1, 1.0, 0.0)
    return sel(pred(vals[15]), 16.0, cnt)


def _dup_bf16_words(v):
    bits = lax.bitcast_convert_type(v.astype(_BF16).astype(_F32), jnp.uint32)
    return bits | (bits >> 16)


def _retrieve_kernel(ht_ref, htn_ref, wq_ref, keys_ref, cnt1_ref, e1_ref, rho2_ref, e2_ref,
                     sa_buf, sb_buf, q_buf, x_buf, *, tn, chunk):
    step = pl.program_id(0)
    half_rows = PEER_HEADS * PEER_HALF

    def mm_piece(c, ht, s_dst):
        if c < 2:
            rs = slice(c * half_rows, (c + 1) * half_rows)
            q_buf[rs, :] = _dot_f32_lhs(wq_ref[rs, :], ht[...]).astype(_BF16)
        else:
            p = c - 2
            s_dst[p] = _dot_f32_lhs(keys_ref[p], q_buf[p * half_rows:(p + 1) * half_rows, :])

    @pl.when(step == 0)
    def _():
        for c in range(4):
            mm_piece(c, ht_ref, sa_buf)

    neg = jnp.full((PEER_HEADS, chunk), -jnp.inf, _F32)

    def slab(k):
        return pl.ds(pl.multiple_of(k * PEER_HEADS, PEER_HEADS), PEER_HEADS)

    def rank_chunk(c, s_buf, s_nxt):
        ls = slice(c * chunk, (c + 1) * chunk)
        a = _top16_desc([s_buf[0, k * PEER_HEADS:(k + 1) * PEER_HEADS, ls] for k in range(PEER_NKEYS)])
        b = _top16_desc([s_buf[1, k * PEER_HEADS:(k + 1) * PEER_HEADS, ls] for k in range(PEER_NKEYS)])
        rows = {}
        for i, j in _CAND:
            rows.setdefault(i, []).append(a[i] + b[j])
        top = rows[0]
        rest = [v for i in range(1, PEER_TOPK) for v in rows[i]]
        while rest:
            grp, rest = rest[:PEER_TOPK], rest[PEER_TOPK:]
            grp = grp + [neg] * (PEER_TOPK - len(grp))
            top = _merge_top16(top, _sort16_desc(grp))
        c16 = top[PEER_TOPK - 1]
        ea = [jnp.exp(a[i] - a[0]) for i in range(PEER_TOPK)]
        eb = [jnp.exp(b[j] - b[0]) for j in range(PEER_TOPK)]
        z = jnp.zeros((PEER_HEADS, chunk), _F32)
        for i, j in _CAND:
            z = z + jnp.where(a[i] + b[j] >= c16, ea[i] * eb[j], 0.0)
        zinv = 1.0 / z

        mm_piece(c, htn_ref, s_nxt)

        def key_body(kk, _):
            for u in range(KEY_UNROLL):
                k = kk * KEY_UNROLL + u
                s1 = s_buf[0, slab(k), ls]
                s2 = s_buf[1, slab(k), ls]
                cnt = _prefix_count(b, lambda v: s1 + v >= c16)
                rho = _prefix_count(b, lambda v: v > s2)
                x_buf[0, slab(k), :] = lax.bitcast_convert_type(_dup_bf16_words(cnt), _F32)
                x_buf[1, slab(k), :] = lax.bitcast_convert_type(
                    _dup_bf16_words(jnp.exp(s1 - a[0]) * zinv), _F32)
                x_buf[2, slab(k), :] = rho
                x_buf[3, slab(k), :] = jnp.exp(s2 - b[0])
            return 0

        lax.fori_loop(0, PEER_NKEYS // KEY_UNROLL, key_body, 0)

        for h in range(PEER_HEADS):
            rows_h = pl.ds(h, PEER_NKEYS, stride=PEER_HEADS)
            cnt1_ref[h, :, ls] = lax.bitcast_convert_type(x_buf[0, rows_h, :], jnp.uint32)
            e1_ref[h, :, ls] = lax.bitcast_convert_type(x_buf[1, rows_h, :], jnp.uint32)
            rho2_ref[h, :, ls] = x_buf[2, rows_h, :].astype(_BF16)
            e2_ref[h, :, ls] = x_buf[3, rows_h, :].astype(_BF16)

    @pl.when(step % 2 == 0)
    def _():
        for c in range(tn // chunk):
            rank_chunk(c, sa_buf, sb_buf)

    @pl.when(step % 2 == 1)
    def _():
        for c in range(tn // chunk):
            rank_chunk(c, sb_buf, sa_buf)


def _retrieve(h_t, wq_t, keys_kh, *, tn, chunk=LANES):
    N = h_t.shape[1]
    assert N % tn == 0 and tn == 4 * chunk and chunk == LANES
    n_t = N // tn
    sds = lambda dt: jax.ShapeDtypeStruct((PEER_HEADS, PEER_NKEYS, N), dt)
    out_spec = pl.BlockSpec((PEER_HEADS, PEER_NKEYS, tn), lambda i: (0, 0, i))
    s_shape = (2, PEER_NKEYS * PEER_HEADS, tn)
    return pl.pallas_call(
        functools.partial(_retrieve_kernel, tn=tn, chunk=chunk),
        grid=(n_t,),
        in_specs=[
            pl.BlockSpec((D_MODEL, tn), lambda i: (0, i)),
            pl.BlockSpec((D_MODEL, tn), lambda i: (0, jnp.minimum(i + 1, n_t - 1))),
            pl.BlockSpec(wq_t.shape, lambda i: (0, 0), pipeline_mode=pl.Buffered(1)),
            pl.BlockSpec(keys_kh.shape, lambda i: (0, 0, 0), pipeline_mode=pl.Buffered(1)),
        ],
        out_specs=[out_spec] * 4,
        out_shape=[sds(jnp.uint32), sds(jnp.uint32), sds(_BF16), sds(_BF16)],
        scratch_shapes=[
            pltpu.VMEM(s_shape, _F32),
            pltpu.VMEM(s_shape, _F32),
            pltpu.VMEM((2 * PEER_HEADS * PEER_HALF, tn), _BF16),
            pltpu.VMEM((4, PEER_NKEYS * PEER_HEADS, chunk), _F32),
        ],
        compiler_params=pltpu.CompilerParams(
            dimension_semantics=("arbitrary",), vmem_limit_bytes=VMEM_LIMIT),
        name="peer_retrieve",
    )(h_t, h_t, wq_t, keys_kh)


def _experts_kernel(ht_ref, h1_ref, u_ref, vt_ref, cnt1_ref, e1_ref, rho2_ref, e2_ref, g_ref, b_ref,
                    out_ref, z0_buf, z1_buf, gz0_buf, gz1_buf, acc_buf, *, tn, te, n_e, lane_chunk):
    j = pl.program_id(0)
    rows = te // PEER_NKEYS
    e_prev = (j + n_e - 1) % n_e
    e_pp = (j + n_e - 2) % n_e

    @pl.when(j == 0)
    def _():
        z1_buf[...] = jnp.zeros_like(z1_buf)
        gz0_buf[...] = jnp.zeros_like(gz0_buf)
        gz1_buf[...] = jnp.zeros_like(gz1_buf)

    @pl.when(jnp.logical_or(e_pp == 0, j == 0))
    def _():
        acc_buf[...] = jnp.zeros_like(acc_buf)

    def stages(z_new, z_cur, gz_cur, gz_old):
        k1_base = pl.multiple_of(e_prev * rows, rows)
        zero = jnp.zeros((PACK, lane_chunk), _BF16)

        def weights(r_lo, r_hi):
            for c in range(tn // lane_chunk):
                ls = slice(c * lane_chunk, (c + 1) * lane_chunk)
                cnt1 = [cnt1_ref[h, pl.ds(k1_base, rows), ls] for h in range(PEER_HEADS)]
                e1 = [e1_ref[h, pl.ds(k1_base, rows), ls] for h in range(PEER_HEADS)]
                for r in range(r_lo, r_hi):
                    def bcast(w):
                        w8 = jnp.broadcast_to(w[r:r + 1, :], (SUBLANES, lane_chunk))
                        return pltpu.bitcast(w8, _BF16)
                    n_q = PEER_NKEYS // PACK
                    g = [None] * n_q
                    for h in range(PEER_HEADS):
                        cb = bcast(cnt1[h])
                        eb = bcast(e1[h])
                        for q in range(n_q):
                            ks = slice(q * PACK, (q + 1) * PACK)
                            m = rho2_ref[h, ks, ls] < cb
                            term = jnp.where(m, e2_ref[h, ks, ls], zero) * eb
                            g[q] = term if g[q] is None else g[q] + term
                    for q in range(n_q):
                        rs = slice(r * PEER_NKEYS + q * PACK, r * PEER_NKEYS + (q + 1) * PACK)
                        zz = z_cur[rs, ls].astype(_BF16)
                        gz_cur[rs, ls] = g[q] * _gelu_sig(zz)

        weights(0, rows)
        z_new[...] = _dot_f32_lhs(u_ref[...], ht_ref[...])
        acc_buf[...] += _dot_f32_lhs(vt_ref[...], gz_old[...])

    @pl.when(j % 2 == 0)
    def _():
        stages(z0_buf, z1_buf, gz1_buf, gz0_buf)

    @pl.when(j % 2 == 1)
    def _():
        stages(z1_buf, z0_buf, gz0_buf, gz1_buf)

    @pl.when(jnp.logical_and(e_pp == n_e - 1, j >= 2))
    def _():
        ffn = acc_buf[...].T
        out_ref[...] = _layer_norm(ALPHA * h1_ref[...] + ffn, g_ref[...], b_ref[...])


def _experts(h_t, h1, u_b, vt_b, cnt1, e1, rho2, e2, ln_g, ln_b, *, tn, te, lane_chunk=256):
    N = h1.shape[0]
    assert N % tn == 0 and N_EXPERTS % te == 0 and te % (SUBLANES * PEER_NKEYS) == 0
    n_e = N_EXPERTS // te
    total = (N // tn) * n_e
    last = total - 1
    tile0 = lambda j: jnp.minimum(j, last)
    tile1 = lambda j: jnp.clip(j - 1, 0, last)
    tile2 = lambda j: jnp.clip(j - 2, 0, last)
    sel_spec = pl.BlockSpec((PEER_HEADS, PEER_NKEYS, tn), lambda j: (0, 0, tile1(j) // n_e))
    return pl.pallas_call(
        functools.partial(_experts_kernel, tn=tn, te=te, n_e=n_e, lane_chunk=min(lane_chunk, tn)),
        grid=(total + 2,),
        in_specs=[
            pl.BlockSpec((D_MODEL, tn), lambda j: (0, tile0(j) // n_e)),
            pl.BlockSpec((tn, D_MODEL), lambda j: (tile2(j) // n_e, 0)),
            pl.BlockSpec((te, D_MODEL), lambda j: (tile0(j) % n_e, 0)),
            pl.BlockSpec((D_MODEL, te), lambda j: (0, tile2(j) % n_e)),
            sel_spec, sel_spec, sel_spec, sel_spec,
            pl.BlockSpec((1, D_MODEL), lambda j: (0, 0)),
            pl.BlockSpec((1, D_MODEL), lambda j: (0, 0)),
        ],
        out_specs=pl.BlockSpec((tn, D_MODEL), lambda j: (tile2(j) // n_e, 0)),
        out_shape=jax.ShapeDtypeStruct((N, D_MODEL), _F32),
        scratch_shapes=[
            pltpu.VMEM((te, tn), _F32),
            pltpu.VMEM((te, tn), _F32),
            pltpu.VMEM((te, tn), _BF16),
            pltpu.VMEM((te, tn), _BF16),
            pltpu.VMEM((D_MODEL, tn), _F32),
        ],
        compiler_params=pltpu.CompilerParams(
            dimension_semantics=("arbitrary",), vmem_limit_bytes=VMEM_LIMIT),
        name="peer_experts",
    )(h_t, h1, u_b, vt_b, cnt1, e1, rho2, e2, ln_g, ln_b)


def _pick_tile(n, target):
    t = min(n, target)
    while n % t:
        t //= 2
    return t


def kernel(x, meta, ln_in_g, ln_in_b, w_in, conv_w, conv_b, w_rg, b_rg, w_ig, b_ig, lru_L, w_proj_a, pool_w, pool_scale, w_proj_b, w_out, ln1_g, ln1_b, w_q, sub_keys, expert_u, expert_v, ln2_g, ln2_b):
    B, S, D = x.shape
    assert D == D_MODEL and w_in.shape[0] == DEPTH
    row = lambda v: v.reshape(1, -1).astype(_F32)
    l = 0
    mixer_weights = (
        row(ln_in_g), row(ln_in_b), w_in[l].astype(_BF16), conv_w[l].astype(_F32), row(conv_b[l]),
        jnp.concatenate([w_rg[l], w_ig[l]], axis=-1).astype(_BF16), row(b_rg[l]), row(b_ig[l]),
        row(lru_L[l]), w_proj_a[l].astype(_BF16), pool_w[l].astype(_BF16), row(pool_scale[l]),
        w_proj_b[l].astype(_BF16), w_out[l].astype(_BF16), row(ln1_g[l]), row(ln1_b[l]),
    )
    zero_state = (jnp.zeros((CONV_TAIL, D_RNN), _F32), jnp.zeros((POOL_TAIL, D_POOL), _F32),
                  jnp.zeros((SUBLANES, D_RNN), _F32))
    _, xr_t, xp_t, hs = _mixer(meta[None].astype(_F32), mixer_weights, zero_state,
                               tt=N_META, pos0=0, emit_state=True)
    h1, h_t = _mixer(x, mixer_weights, (xr_t, xp_t, hs), tt=_pick_tile(S, 256), pos0=N_META,
                     emit_state=False)
    h1 = h1.reshape(B * S, D)
    N = B * S

    wq_t = w_q[l].reshape(D, PEER_HEADS, 2, PEER_HALF).transpose(2, 1, 3, 0)
    wq_t = wq_t.reshape(2 * PEER_HEADS * PEER_HALF, D)
    eye = jnp.eye(PEER_HEADS, dtype=_F32)
    keys_kh = jnp.einsum("hpkd,hg->pkhgd", sub_keys[l], eye)
    keys_kh = keys_kh.reshape(2, PEER_NKEYS * PEER_HEADS, PEER_HEADS * PEER_HALF)
    tn = _pick_tile(N, 512)
    cnt1, e1, rho2, e2 = _retrieve(h_t, wq_t, keys_kh, tn=tn)

    u_b = expert_u[l]
    vt_b = expert_v[l].T
    out = _experts(h_t, h1, u_b, vt_b, cnt1, e1, rho2, e2, row(ln2_g[l]), row(ln2_b[l]),
                   tn=tn, te=1024)
    return out.reshape(B, S, D).astype(x.dtype)
```

```python
import functools

import jax
import jax.numpy as jnp
from jax import lax
from jax.experimental import pallas as pl
from jax.experimental.pallas import tpu as pltpu

D_MODEL = 1024
N_META = 16
D_RNN = D_MODEL
N_RNN_BLOCKS = 4
RNN_BLOCK = D_RNN // N_RNN_BLOCKS
CONV_WIDTH = 4
LRU_C = 8.0
POOL_WINDOWS = (2, 4, 8, 16)
D_POOL = D_MODEL // 2
POOL_GROUP = D_POOL // len(POOL_WINDOWS)
PEER_HEADS = 8
PEER_NKEYS = 128
N_EXPERTS = PEER_NKEYS * PEER_NKEYS
PEER_HALF = 128
PEER_TOPK = 16
DEPTH = 1
ALPHA = (2.0 * DEPTH) ** 0.25
LN_EPS = 1e-5

SUBLANES = 8
PACK = 16
KEY_UNROLL = 4
LANES = 128
CONV_TAIL = SUBLANES
POOL_TAIL = 16
VMEM_LIMIT = 56 * 1024 * 1024

_F32 = jnp.float32
_BF16 = jnp.bfloat16


def _layer_norm(v, g, b):
    mu = jnp.mean(v, axis=-1, keepdims=True)
    vc = v - mu
    var = jnp.mean(vc * vc, axis=-1, keepdims=True)
    return vc * lax.rsqrt(var + LN_EPS) * g + b


def _gelu_sig(v):
    w = v * (-2.3022082870680315 - 0.10294324120074478 * (v * v))
    return v / (1.0 + jnp.exp2(w))


def _sigmoid(v):
    return 1.0 / (1.0 + jnp.exp(-v))


def _dot_f32_lhs(a, b):
    return lax.dot_general(a, b, (((1,), (0,)), ((), ())), preferred_element_type=_F32)


def _lru_scan(a, b, h_prev):
    tt, c = a.shape
    groups = tt // SUBLANES
    a3 = a.reshape(groups, SUBLANES, c)
    b3 = b.reshape(groups, SUBLANES, c)
    row = lax.broadcasted_iota(jnp.int32, a3.shape, 1)
    shift = 1
    while shift < SUBLANES:
        a_sh = pltpu.roll(a3, shift, axis=1)
        b_sh = pltpu.roll(b3, shift, axis=1)
        m = row >= shift
        b3 = jnp.where(m, a3 * b_sh + b3, b3)
        a3 = jnp.where(m, a3 * a_sh, a3)
        shift *= 2
    outs = []
    carry = h_prev
    for g in range(groups):
        hg = a3[g] * carry + b3[g]
        outs.append(hg)
        carry = hg[SUBLANES - 1:SUBLANES, :]
    return jnp.concatenate(outs, axis=0)


def _mixer_kernel(x_ref, lng_ref, lnb_ref, win_ref, convw_ref, convb_ref, wgate_ref, brg_ref,
                  big_ref, lrul_ref, wpa_ref, poolw_ref, pscale_ref, wpb_ref, wout_ref,
                  ln1g_ref, ln1b_ref, xr0_ref, xp0_ref, hs0_ref,
                  h1_ref, *rest, tt, pos0, emit_state, nb):
    xr_all, xp_all, hs_all = rest[-5:-2]

    @pl.when(pl.program_id(1) == 0)
    def _():
        for bi in range(nb):
            xr_all[bi, 0:CONV_TAIL, :] = xr0_ref[...]
            xp_all[bi, 0:POOL_TAIL, :] = xp0_ref[...]
            hs_all[bi] = hs0_ref[...]

    tails = [
        _mixer_tile(bi, x_ref, lng_ref, lnb_ref, win_ref, convw_ref, convb_ref, wgate_ref, brg_ref,
                    big_ref, lrul_ref, wpa_ref, poolw_ref, pscale_ref, wpb_ref, wout_ref,
                    ln1g_ref, ln1b_ref, h1_ref, rest, tt=tt, pos0=pos0, emit_state=emit_state)
        for bi in range(nb)]
    for tail in tails:
        tail()


def _mixer_tile(bi, x_ref, lng_ref, lnb_ref, win_ref, convw_ref, convb_ref, wgate_ref, brg_ref,
                big_ref, lrul_ref, wpa_ref, poolw_ref, pscale_ref, wpb_ref, wout_ref,
                ln1g_ref, ln1b_ref, h1_ref, rest, *, tt, pos0, emit_state):
    if emit_state:
        xr_out, xp_out, hs_out, xr_all, xp_all, hs_all, rg_all, mx_all = rest
    else:
        ht_ref, xr_all, xp_all, hs_all, rg_all, mx_all = rest
    xr_buf, xp_buf, hs_buf, rg_buf, mx_buf = (r.at[bi] for r in (xr_all, xp_all, hs_all, rg_all, mx_all))
    t = pl.program_id(1)
    x = x_ref[bi]
    h0 = _layer_norm(x, lng_ref[...], lnb_ref[...])

    xr_buf[CONV_TAIL:CONV_TAIL + tt, :] = _dot_f32_lhs(h0, win_ref[:, 0:D_RNN])
    lsig = lrul_ref[...]
    log_sig = jnp.minimum(lsig, 0.0) - jnp.log1p(jnp.exp(-jnp.abs(lsig)))
    for n in range(N_RNN_BLOCKS):
        cs = slice(n * RNN_BLOCK, (n + 1) * RNN_BLOCK)
        xc = convb_ref[:, cs]
        for j in range(CONV_WIDTH):
            off = CONV_TAIL - (CONV_WIDTH - 1) + j
            xc = xc + convw_ref[j:j + 1, cs] * xr_buf[off:off + tt, cs]
        pre = _dot_f32_lhs(xc, wgate_ref[n])
        r = _sigmoid(pre[:, :RNN_BLOCK] + brg_ref[:, cs])
        i = _sigmoid(pre[:, RNN_BLOCK:] + big_ref[:, cs])
        log_a = (LRU_C * log_sig[:, cs]) * r
        a = jnp.exp(log_a)
        mult = jnp.sqrt(1.0 - a * a)
        h = _lru_scan(a, mult * i * xc, hs_buf[0:1, cs])
        hs_buf[:, cs] = jnp.broadcast_to(h[tt - 1:tt, :], (SUBLANES, RNN_BLOCK))
        gate = _dot_f32_lhs(h0, win_ref[:, D_RNN + n * RNN_BLOCK:D_RNN + (n + 1) * RNN_BLOCK])
        rg_buf[:, cs] = h * _gelu_sig(gate)

    def tail():
        y_a = _dot_f32_lhs(rg_buf[...], wpa_ref[...])

        xp_buf[POOL_TAIL:POOL_TAIL + tt, :] = _dot_f32_lhs(h0, win_ref[:, 2 * D_RNN:2 * D_RNN + D_POOL])
        if pos0 < max(POOL_WINDOWS):
            pos = (lax.broadcasted_iota(jnp.int32, (tt, 1), 0) + (pos0 + 1) + t * tt).astype(_F32)
        for g, w in enumerate(POOL_WINDOWS):
            cs = slice(g * POOL_GROUP, (g + 1) * POOL_GROUP)
            cur = xp_buf[POOL_TAIL:POOL_TAIL + tt, cs]
            acc = cur
            for j in range(1, w):
                acc = acc + xp_buf[POOL_TAIL - j:POOL_TAIL - j + tt, cs]
            if pos0 < max(POOL_WINDOWS):
                mean = acc / jnp.minimum(pos, float(w))
            else:
                mean = acc * (1.0 / w)
            mixed = _dot_f32_lhs(mean - cur, poolw_ref[g]) * pscale_ref[:, cs]
            mx_buf[:, cs] = mixed
        y_b = _dot_f32_lhs(mx_buf[...], wpb_ref[...])

        g_off = 2 * D_RNN + D_POOL
        gate_a = _sigmoid(_dot_f32_lhs(h0, win_ref[:, g_off:g_off + D_MODEL]))
        gate_b = _sigmoid(_dot_f32_lhs(h0, win_ref[:, g_off + D_MODEL:g_off + 2 * D_MODEL]))
        merged = gate_a * y_a + gate_b * y_b
        mix = _dot_f32_lhs(merged, wout_ref[...])
        h1 = _layer_norm(ALPHA * h0 + mix, ln1g_ref[...], ln1b_ref[...])
        h1_ref[bi] = h1
        if not emit_state:
            ht_ref[bi] = h1.T.astype(_BF16)

        xr_buf[0:CONV_TAIL, :] = xr_buf[tt:tt + CONV_TAIL, :]
        xp_buf[0:POOL_TAIL, :] = xp_buf[tt:tt + POOL_TAIL, :]
        if emit_state:
            xr_out[...] = xr_buf[0:CONV_TAIL, :]
            xp_out[...] = xp_buf[0:POOL_TAIL, :]
            hs_out[...] = hs_buf[...]

    return tail


def _const_spec(shape):
    nd = len(shape)
    return pl.BlockSpec(shape, lambda b, t: (0,) * nd, pipeline_mode=pl.Buffered(1))


def _mixer(x, weights, state, *, tt, pos0, emit_state, nb=1):
    B, S, D = x.shape
    assert S % tt == 0 and tt % POOL_TAIL == 0 and B % nb == 0
    in_specs = [pl.BlockSpec((nb, tt, D), lambda b, t: (b, t, 0))]
    in_specs += [_const_spec(w.shape) for w in weights]
    in_specs += [_const_spec(s.shape) for s in state]
    out_shape = [jax.ShapeDtypeStruct((B, S, D), _F32)]
    out_specs = [pl.BlockSpec((nb, tt, D), lambda b, t: (b, t, 0))]
    if emit_state:
        assert B == 1 and S == tt
        out_shape += [jax.ShapeDtypeStruct(s.shape, _F32) for s in state]
        out_specs += [pl.BlockSpec(s.shape, lambda b, t: (0, 0)) for s in state]
    else:
        assert tt % LANES == 0
        out_shape += [jax.ShapeDtypeStruct((B, D, S), _BF16)]
        out_specs += [pl.BlockSpec((nb, D, tt), lambda b, t: (b, 0, t))]
    scratch = [
        pltpu.VMEM((nb, CONV_TAIL + tt, D_RNN), _F32),
        pltpu.VMEM((nb, POOL_TAIL + tt, D_POOL), _F32),
        pltpu.VMEM((nb, SUBLANES, D_RNN), _F32),
        pltpu.VMEM((nb, tt, D_RNN), _F32),
        pltpu.VMEM((nb, tt, D_POOL), _F32),
    ]
    return pl.pallas_call(
        functools.partial(_mixer_kernel, tt=tt, pos0=pos0, emit_state=emit_state, nb=nb),
        grid=(B // nb, S // tt),
        in_specs=in_specs,
        out_specs=out_specs,
        out_shape=out_shape,
        scratch_shapes=scratch,
        compiler_params=pltpu.CompilerParams(
            dimension_semantics=("arbitrary", "arbitrary"), vmem_limit_bytes=VMEM_LIMIT),
        name="mixer_meta" if emit_state else "mixer",
    )(x, *weights, *state)


def _oddeven_merge_sort_pairs(n):
    pairs = []

    def merge(lo, hi, r):
        step = r * 2
        if step < hi - lo:
            merge(lo, hi, step)
            merge(lo + r, hi, step)
            for i in range(lo + r, hi - r, step):
                pairs.append((i, i + r))
        else:
            pairs.append((lo, lo + r))

    def sort(lo, hi):
        if hi - lo >= 1:
            mid = lo + (hi - lo) // 2
            sort(lo, mid)
            sort(mid + 1, hi)
            merge(lo, hi, 1)

    sort(0, n - 1)
    return pairs


_SORT16 = _oddeven_merge_sort_pairs(PEER_TOPK)


def _cmpx(v, i, j):
    hi = jnp.maximum(v[i], v[j])
    lo = jnp.minimum(v[i], v[j])
    v[i] = hi
    v[j] = lo


def _sort16_desc(v):
    v = list(v)
    for i, j in _SORT16:
        _cmpx(v, i, j)
    return v


def _merge_top16(a, b):
    c = list(a)
    for k in range(len(b)):
        c[PEER_TOPK - 1 - k] = jnp.maximum(a[PEER_TOPK - 1 - k], b[k])
    stride = PEER_TOPK // 2
    while stride >= 1:
        for i in range(PEER_TOPK):
            if i & stride == 0:
                _cmpx(c, i, i + stride)
        stride //= 2
    return c


def _top16_desc(vals):
    runs = [_sort16_desc(vals[i:i + PEER_TOPK]) for i in range(0, len(vals), PEER_TOPK)]
    while len(runs) > 1:
        runs = [_merge_top16(runs[i], runs[i + 1]) for i in range(0, len(runs), 2)]
    return runs[0]


_CAND = [(i, j) for i in range(PEER_TOPK) for j in range(PEER_TOPK) if (i + 1) * (j + 1) <= PEER_TOPK]


def _prefix_count(vals, pred):
    sel = jnp.where
    m8 = pred(vals[7])
    m4 = pred(sel(m8, vals[11], vals[3]))
    m2 = pred(sel(m8, sel(m4, vals[13], vals[9]), sel(m4, vals[5], vals[1])))
    lo = sel(m4, sel(m2, vals[6], vals[4]), sel(m2, vals[2], vals[0]))
    hi = sel(m4, sel(m2, vals[14], vals[12]), sel(m2, vals[10], vals[8]))
    m1 = pred(sel(m8, hi, lo))
    cnt = sel(m8, 8.0, 0.0) + sel(m4, 4.0, 0.0) + sel(m2, 2.0, 0.0) + sel(m1, 1.0, 0.0)
    return sel(pred(vals[15]), 16.0, cnt)


def _dup_bf16_words(v):
    bits = lax.bitcast_convert_type(v.astype(_BF16).astype(_F32), jnp.uint32)
    return bits | (bits >> 16)


def _retrieve_kernel(ht_ref, htn_ref, wq_ref, keys_ref, cnt1_ref, e1_ref, rho2_ref, e2_ref,
                     sa_buf, sb_buf, q_buf, x_buf, *, tn, chunk):
    step = pl.program_id(0)
    half_rows = PEER_HEADS * PEER_HALF

    def mm_piece(c, ht, s_dst):
        if c < 2:
            rs = slice(c * half_rows, (c + 1) * half_rows)
            q_buf[rs, :] = _dot_f32_lhs(wq_ref[rs, :], ht[0]).astype(_BF16)
        else:
            p = c - 2
            s_dst[p] = _dot_f32_lhs(keys_ref[p], q_buf[p * half_rows:(p + 1) * half_rows, :])

    @pl.when(step == 0)
    def _():
        for c in range(4):
            mm_piece(c, ht_ref, sa_buf)

    neg = jnp.full((PEER_HEADS, chunk), -jnp.inf, _F32)

    def slab(k):
        return pl.ds(pl.multiple_of(k * PEER_HEADS, PEER_HEADS), PEER_HEADS)

    def rank_chunk(c, s_buf, s_nxt):
        ls = slice(c * chunk, (c + 1) * chunk)
        a = _top16_desc([s_buf[0, k * PEER_HEADS:(k + 1) * PEER_HEADS, ls] for k in range(PEER_NKEYS)])
        b = _top16_desc([s_buf[1, k * PEER_HEADS:(k + 1) * PEER_HEADS, ls] for k in range(PEER_NKEYS)])
        rows = {}
        for i, j in _CAND:
            rows.setdefault(i, []).append(a[i] + b[j])
        top = rows[0]
        rest = [v for i in range(1, PEER_TOPK) for v in rows[i]]
        while rest:
            grp, rest = rest[:PEER_TOPK], rest[PEER_TOPK:]
            grp = grp + [neg] * (PEER_TOPK - len(grp))
            top = _merge_top16(top, _sort16_desc(grp))
        c16 = top[PEER_TOPK - 1]
        ea = [jnp.exp(a[i] - a[0]) for i in range(PEER_TOPK)]
        eb = [jnp.exp(b[j] - b[0]) for j in range(PEER_TOPK)]
        z = jnp.zeros((PEER_HEADS, chunk), _F32)
        for i, j in _CAND:
            z = z + jnp.where(a[i] + b[j] >= c16, ea[i] * eb[j], 0.0)
        zinv = 1.0 / z

        mm_piece(c, htn_ref, s_nxt)

        def key_body(kk, _):
            for u in range(KEY_UNROLL):
                k = kk * KEY_UNROLL + u
                s1 = s_buf[0, slab(k), ls]
                s2 = s_buf[1, slab(k), ls]
                cnt = _prefix_count(b, lambda v: s1 + v >= c16)
                rho = _prefix_count(b, lambda v: v > s2)
                x_buf[0, slab(k), :] = lax.bitcast_convert_type(_dup_bf16_words(cnt), _F32)
                x_buf[1, slab(k), :] = lax.bitcast_convert_type(
                    _dup_bf16_words(jnp.exp(s1 - a[0]) * zinv), _F32)
                x_buf[2, slab(k), :] = rho
                x_buf[3, slab(k), :] = jnp.exp(s2 - b[0])
            return 0

        lax.fori_loop(0, PEER_NKEYS // KEY_UNROLL, key_body, 0)

        for h in range(PEER_HEADS):
            rows_h = pl.ds(h, PEER_NKEYS, stride=PEER_HEADS)
            cnt1_ref[h, :, ls] = lax.bitcast_convert_type(x_buf[0, rows_h, :], jnp.uint32)
            e1_ref[h, :, ls] = lax.bitcast_convert_type(x_buf[1, rows_h, :], jnp.uint32)
            rho2_ref[h, :, ls] = x_buf[2, rows_h, :].astype(_BF16)
            e2_ref[h, :, ls] = x_buf[3, rows_h, :].astype(_BF16)

    @pl.when(step % 2 == 0)
    def _():
        for c in range(tn // chunk):
            rank_chunk(c, sa_buf, sb_buf)

    @pl.when(step % 2 == 1)
    def _():
        for c in range(tn // chunk):
            rank_chunk(c, sb_buf, sa_buf)


def _retrieve(h_t, wq_t, keys_kh, *, tn, chunk=LANES):
    B, _, S = h_t.shape
    N = B * S
    assert S % tn == 0 and tn == 4 * chunk and chunk == LANES
    n_t = N // tn
    nst = S // tn
    ht_spec = lambda f: pl.BlockSpec((1, D_MODEL, tn), lambda i: (f(i) // nst, 0, f(i) % nst))
    sds = lambda dt: jax.ShapeDtypeStruct((PEER_HEADS, PEER_NKEYS, N), dt)
    out_spec = pl.BlockSpec((PEER_HEADS, PEER_NKEYS, tn), lambda i: (0, 0, i))
    s_shape = (2, PEER_NKEYS * PEER_HEADS, tn)
    return pl.pallas_call(
        functools.partial(_retrieve_kernel, tn=tn, chunk=chunk),
        grid=(n_t,),
        in_specs=[
            ht_spec(lambda i: i),
            ht_spec(lambda i: jnp.minimum(i + 1, n_t - 1)),
            pl.BlockSpec(wq_t.shape, lambda i: (0, 0), pipeline_mode=pl.Buffered(1)),
            pl.BlockSpec(keys_kh.shape, lambda i: (0, 0, 0), pipeline_mode=pl.Buffered(1)),
        ],
        out_specs=[out_spec] * 4,
        out_shape=[sds(jnp.uint32), sds(jnp.uint32), sds(_BF16), sds(_BF16)],
        scratch_shapes=[
            pltpu.VMEM(s_shape, _F32),
            pltpu.VMEM(s_shape, _F32),
            pltpu.VMEM((2 * PEER_HEADS * PEER_HALF, tn), _BF16),
            pltpu.VMEM((4, PEER_NKEYS * PEER_HEADS, chunk), _F32),
        ],
        compiler_params=pltpu.CompilerParams(
            dimension_semantics=("arbitrary",), vmem_limit_bytes=VMEM_LIMIT),
        name="peer_retrieve",
    )(h_t, h_t, wq_t, keys_kh)


def _experts_kernel(ht_ref, h1_ref, u_ref, vt_ref, cnt1_ref, e1_ref, rho2_ref, e2_ref, g_ref, b_ref,
                    out_ref, z0_buf, z1_buf, gz0_buf, gz1_buf, acc_buf, *, tn, te, n_e, lane_chunk):
    j = pl.program_id(0)
    rows = te // PEER_NKEYS
    e_prev = (j + n_e - 1) % n_e
    e_pp = (j + n_e - 2) % n_e

    @pl.when(j == 0)
    def _():
        z1_buf[...] = jnp.zeros_like(z1_buf)
        gz0_buf[...] = jnp.zeros_like(gz0_buf)
        gz1_buf[...] = jnp.zeros_like(gz1_buf)

    @pl.when(jnp.logical_or(e_pp == 0, j == 0))
    def _():
        acc_buf[...] = jnp.zeros_like(acc_buf)

    def stages(z_new, z_cur, gz_cur, gz_old):
        k1_base = pl.multiple_of(e_prev * rows, rows)
        zero = jnp.zeros((PACK, lane_chunk), _BF16)

        def weights(r_lo, r_hi):
            for c in range(tn // lane_chunk):
                ls = slice(c * lane_chunk, (c + 1) * lane_chunk)
                cnt1 = [cnt1_ref[h, pl.ds(k1_base, rows), ls] for h in range(PEER_HEADS)]
                e1 = [e1_ref[h, pl.ds(k1_base, rows), ls] for h in range(PEER_HEADS)]
                for r in range(r_lo, r_hi):
                    def bcast(w):
                        w8 = jnp.broadcast_to(w[r:r + 1, :], (SUBLANES, lane_chunk))
                        return pltpu.bitcast(w8, _BF16)
                    n_q = PEER_NKEYS // PACK
                    g = [None] * n_q
                    for h in range(PEER_HEADS):
                        cb = bcast(cnt1[h])
                        eb = bcast(e1[h])
                        for q in range(n_q):
                            ks = slice(q * PACK, (q + 1) * PACK)
                            m = rho2_ref[h, ks, ls] < cb
                            term = jnp.where(m, e2_ref[h, ks, ls], zero) * eb
                            g[q] = term if g[q] is None else g[q] + term
                    for q in range(n_q):
                        rs = slice(r * PEER_NKEYS + q * PACK, r * PEER_NKEYS + (q + 1) * PACK)
                        zz = z_cur[rs, ls].astype(_BF16)
                        gz_cur[rs, ls] = g[q] * _gelu_sig(zz)

        weights(0, rows)
        z_new[...] = _dot_f32_lhs(u_ref[...], ht_ref[0])
        acc_buf[...] += _dot_f32_lhs(vt_ref[...], gz_old[...])

    @pl.when(j % 2 == 0)
    def _():
        stages(z0_buf, z1_buf, gz1_buf, gz0_buf)

    @pl.when(j % 2 == 1)
    def _():
        stages(z1_buf, z0_buf, gz0_buf, gz1_buf)

    @pl.when(jnp.logical_and(e_pp == n_e - 1, j >= 2))
    def _():
        ffn = acc_buf[...].T
        out_ref[...] = _layer_norm(ALPHA * h1_ref[...] + ffn, g_ref[...], b_ref[...])


def _experts(h_t, h1, u_b, vt_b, cnt1, e1, rho2, e2, ln_g, ln_b, *, tn, te, lane_chunk=256):
    N = h1.shape[0]
    nst = h_t.shape[2] // tn
    assert h_t.shape[2] % tn == 0 and N_EXPERTS % te == 0 and te % (SUBLANES * PEER_NKEYS) == 0
    n_e = N_EXPERTS // te
    total = (N // tn) * n_e
    last = total - 1
    tile0 = lambda j: jnp.minimum(j, last)
    tile1 = lambda j: jnp.clip(j - 1, 0, last)
    tile2 = lambda j: jnp.clip(j - 2, 0, last)
    sel_spec = pl.BlockSpec((PEER_HEADS, PEER_NKEYS, tn), lambda j: (0, 0, tile1(j) // n_e))
    return pl.pallas_call(
        functools.partial(_experts_kernel, tn=tn, te=te, n_e=n_e, lane_chunk=min(lane_chunk, tn)),
        grid=(total + 2,),
        in_specs=[
            pl.BlockSpec((1, D_MODEL, tn),
                         lambda j: ((tile0(j) // n_e) // nst, 0, (tile0(j) // n_e) % nst)),
            pl.BlockSpec((tn, D_MODEL), lambda j: (tile2(j) // n_e, 0)),
            pl.BlockSpec((te, D_MODEL), lambda j: (tile0(j) % n_e, 0)),
            pl.BlockSpec((D_MODEL, te), lambda j: (0, tile2(j) % n_e)),
            sel_spec, sel_spec, sel_spec, sel_spec,
            pl.BlockSpec((1, D_MODEL), lambda j: (0, 0)),
            pl.BlockSpec((1, D_MODEL), lambda j: (0, 0)),
        ],
        out_specs=pl.BlockSpec((tn, D_MODEL), lambda j: (tile2(j) // n_e, 0)),
        out_shape=jax.ShapeDtypeStruct((N, D_MODEL), _F32),
        scratch_shapes=[
            pltpu.VMEM((te, tn), _F32),
            pltpu.VMEM((te, tn), _F32),
            pltpu.VMEM((te, tn), _BF16),
            pltpu.VMEM((te, tn), _BF16),
            pltpu.VMEM((D_MODEL, tn), _F32),
        ],
        compiler_params=pltpu.CompilerParams(
            dimension_semantics=("arbitrary",), vmem_limit_bytes=VMEM_LIMIT),
        name="peer_experts",
    )(h_t, h1, u_b, vt_b, cnt1, e1, rho2, e2, ln_g, ln_b)


def _pick_tile(n, target):
    t = min(n, target)
    while n % t:
        t //= 2
    return t


def kernel(x, meta, ln_in_g, ln_in_b, w_in, conv_w, conv_b, w_rg, b_rg, w_ig, b_ig, lru_L, w_proj_a, pool_w, pool_scale, w_proj_b, w_out, ln1_g, ln1_b, w_q, sub_keys, expert_u, expert_v, ln2_g, ln2_b):
    B, S, D = x.shape
    assert D == D_MODEL and w_in.shape[0] == DEPTH
    row = lambda v: v.reshape(1, -1).astype(_F32)
    l = 0
    mixer_weights = (
        row(ln_in_g), row(ln_in_b), w_in[l].astype(_BF16), conv_w[l].astype(_F32), row(conv_b[l]),
        jnp.concatenate([w_rg[l], w_ig[l]], axis=-1).astype(_BF16), row(b_rg[l]), row(b_ig[l]),
        row(lru_L[l]), w_proj_a[l].astype(_BF16), pool_w[l].astype(_BF16), row(pool_scale[l]),
        w_proj_b[l].astype(_BF16), w_out[l].astype(_BF16), row(ln1_g[l]), row(ln1_b[l]),
    )
    zero_state = (jnp.zeros((CONV_TAIL, D_RNN), _F32), jnp.zeros((POOL_TAIL, D_POOL), _F32),
                  jnp.zeros((SUBLANES, D_RNN), _F32))
    _, xr_t, xp_t, hs = _mixer(meta[None].astype(_F32), mixer_weights, zero_state,
                               tt=N_META, pos0=0, emit_state=True)
    h1, h_t = _mixer(x, mixer_weights, (xr_t, xp_t, hs), tt=_pick_tile(S, 256), pos0=N_META,
                     emit_state=False, nb=2 if B % 2 == 0 else 1)
    h1 = h1.reshape(B * S, D)
    N = B * S

    wq_t = w_q[l].reshape(D, PEER_HEADS, 2, PEER_HALF).transpose(2, 1, 3, 0)
    wq_t = wq_t.reshape(2 * PEER_HEADS * PEER_HALF, D)
    eye = jnp.eye(PEER_HEADS, dtype=_F32)
    keys_kh = jnp.einsum("hpkd,hg->pkhgd", sub_keys[l], eye)
    keys_kh = keys_kh.reshape(2, PEER_NKEYS * PEER_HEADS, PEER_HEADS * PEER_HALF)
    tn = _pick_tile(N, 512)
    cnt1, e1, rho2, e2 = _retrieve(h_t, wq_t, keys_kh, tn=tn)

    u_b = expert_u[l]
    vt_b = expert_v[l].T
    out = _experts(h_t, h1, u_b, vt_b, cnt1, e1, rho2, e2, row(ln2_g[l]), row(ln2_b[l]),
                   tn=tn, te=1024)
    return out.reshape(B, S, D).astype(x.dtype)
```

```python
import functools

import jax
import jax.numpy as jnp
from jax import lax
from jax.experimental import pallas as pl
from jax.experimental.pallas import tpu as pltpu

D_MODEL = 1024
N_META = 16
D_RNN = D_MODEL
N_RNN_BLOCKS = 4
RNN_BLOCK = D_RNN // N_RNN_BLOCKS
CONV_WIDTH = 4
LRU_C = 8.0
POOL_WINDOWS = (2, 4, 8, 16)
D_POOL = D_MODEL // 2
POOL_GROUP = D_POOL // len(POOL_WINDOWS)
PEER_HEADS = 8
PEER_NKEYS = 128
N_EXPERTS = PEER_NKEYS * PEER_NKEYS
PEER_HALF = 128
PEER_TOPK = 16
DEPTH = 1
ALPHA = (2.0 * DEPTH) ** 0.25
LN_EPS = 1e-5

SUBLANES = 8
PACK = 16
KEY_UNROLL = 4
LANES = 128
CONV_TAIL = SUBLANES
POOL_TAIL = 16
VMEM_LIMIT = 56 * 1024 * 1024

_F32 = jnp.float32
_BF16 = jnp.bfloat16


def _layer_norm(v, g, b):
    mu = jnp.mean(v, axis=-1, keepdims=True)
    vc = v - mu
    var = jnp.mean(vc * vc, axis=-1, keepdims=True)
    return vc * lax.rsqrt(var + LN_EPS) * g + b


def _gelu_sig(v):
    w = v * (-2.3022082870680315 - 0.10294324120074478 * (v * v))
    return v / (1.0 + jnp.exp2(w))


def _sigmoid(v):
    return 1.0 / (1.0 + jnp.exp(-v))


def _dot_f32_lhs(a, b):
    return lax.dot_general(a, b, (((1,), (0,)), ((), ())), preferred_element_type=_F32)


def _lru_scan(a, b, h_prev):
    tt, c = a.shape
    groups = tt // SUBLANES
    a3 = a.reshape(groups, SUBLANES, c)
    b3 = b.reshape(groups, SUBLANES, c)
    row = lax.broadcasted_iota(jnp.int32, a3.shape, 1)
    shift = 1
    while shift < SUBLANES:
        a_sh = pltpu.roll(a3, shift, axis=1)
        b_sh = pltpu.roll(b3, shift, axis=1)
        m = row >= shift
        b3 = jnp.where(m, a3 * b_sh + b3, b3)
        a3 = jnp.where(m, a3 * a_sh, a3)
        shift *= 2
    outs = []
    carry = h_prev
    for g in range(groups):
        hg = a3[g] * carry + b3[g]
        outs.append(hg)
        carry = hg[SUBLANES - 1:SUBLANES, :]
    return jnp.concatenate(outs, axis=0)


def _mixer_kernel(x_ref, lng_ref, lnb_ref, win_ref, convw_ref, convb_ref, wgate_ref, brg_ref,
                  big_ref, lrul_ref, wpa_ref, poolw_ref, pscale_ref, wpb_ref, wout_ref,
                  ln1g_ref, ln1b_ref, xr0_ref, xp0_ref, hs0_ref,
                  h1_ref, *rest, tt, pos0, emit_state, nb):
    xr_all, xp_all, hs_all = rest[-5:-2]

    @pl.when(pl.program_id(1) == 0)
    def _():
        for bi in range(nb):
            xr_all[bi, 0:CONV_TAIL, :] = xr0_ref[...]
            xp_all[bi, 0:POOL_TAIL, :] = xp0_ref[...]
            hs_all[bi] = hs0_ref[...]

    tails = [
        _mixer_tile(bi, x_ref, lng_ref, lnb_ref, win_ref, convw_ref, convb_ref, wgate_ref, brg_ref,
                    big_ref, lrul_ref, wpa_ref, poolw_ref, pscale_ref, wpb_ref, wout_ref,
                    ln1g_ref, ln1b_ref, h1_ref, rest, tt=tt, pos0=pos0, emit_state=emit_state)
        for bi in range(nb)]
    for tail in tails:
        tail()


def _mixer_tile(bi, x_ref, lng_ref, lnb_ref, win_ref, convw_ref, convb_ref, wgate_ref, brg_ref,
                big_ref, lrul_ref, wpa_ref, poolw_ref, pscale_ref, wpb_ref, wout_ref,
                ln1g_ref, ln1b_ref, h1_ref, rest, *, tt, pos0, emit_state):
    if emit_state:
        xr_out, xp_out, hs_out, xr_all, xp_all, hs_all, rg_all, mx_all = rest
    else:
        ht_ref, xr_all, xp_all, hs_all, rg_all, mx_all = rest
    xr_buf, xp_buf, hs_buf, rg_buf, mx_buf = (r.at[bi] for r in (xr_all, xp_all, hs_all, rg_all, mx_all))
    t = pl.program_id(1)
    x = x_ref[bi]
    h0 = _layer_norm(x, lng_ref[...], lnb_ref[...])

    xr_buf[CONV_TAIL:CONV_TAIL + tt, :] = _dot_f32_lhs(h0, win_ref[:, 0:D_RNN])
    lsig = lrul_ref[...]
    log_sig = jnp.minimum(lsig, 0.0) - jnp.log1p(jnp.exp(-jnp.abs(lsig)))
    for n in range(N_RNN_BLOCKS):
        cs = slice(n * RNN_BLOCK, (n + 1) * RNN_BLOCK)
        xc = convb_ref[:, cs]
        for j in range(CONV_WIDTH):
            off = CONV_TAIL - (CONV_WIDTH - 1) + j
            xc = xc + convw_ref[j:j + 1, cs] * xr_buf[off:off + tt, cs]
        pre = _dot_f32_lhs(xc, wgate_ref[n])
        r = _sigmoid(pre[:, :RNN_BLOCK] + brg_ref[:, cs])
        i = _sigmoid(pre[:, RNN_BLOCK:] + big_ref[:, cs])
        log_a = (LRU_C * log_sig[:, cs]) * r
        a = jnp.exp(log_a)
        mult = jnp.sqrt(1.0 - a * a)
        h = _lru_scan(a, mult * i * xc, hs_buf[0:1, cs])
        hs_buf[:, cs] = jnp.broadcast_to(h[tt - 1:tt, :], (SUBLANES, RNN_BLOCK))
        gate = _dot_f32_lhs(h0, win_ref[:, D_RNN + n * RNN_BLOCK:D_RNN + (n + 1) * RNN_BLOCK])
        rg_buf[:, cs] = h * _gelu_sig(gate)
    xp_buf[POOL_TAIL:POOL_TAIL + tt, :] = _dot_f32_lhs(h0, win_ref[:, 2 * D_RNN:2 * D_RNN + D_POOL])
    g_off = 2 * D_RNN + D_POOL
    gate_a_pre = _dot_f32_lhs(h0, win_ref[:, g_off:g_off + D_MODEL])
    gate_b_pre = _dot_f32_lhs(h0, win_ref[:, g_off + D_MODEL:g_off + 2 * D_MODEL])

    def tail():
        y_a = _dot_f32_lhs(rg_buf[...], wpa_ref[...])

        if pos0 < max(POOL_WINDOWS):
            pos = (lax.broadcasted_iota(jnp.int32, (tt, 1), 0) + (pos0 + 1) + t * tt).astype(_F32)
        for g, w in enumerate(POOL_WINDOWS):
            cs = slice(g * POOL_GROUP, (g + 1) * POOL_GROUP)
            cur = xp_buf[POOL_TAIL:POOL_TAIL + tt, cs]
            acc = cur
            for j in range(1, w):
                acc = acc + xp_buf[POOL_TAIL - j:POOL_TAIL - j + tt, cs]
            if pos0 < max(POOL_WINDOWS):
                mean = acc / jnp.minimum(pos, float(w))
            else:
                mean = acc * (1.0 / w)
            mixed = _dot_f32_lhs(mean - cur, poolw_ref[g]) * pscale_ref[:, cs]
            mx_buf[:, cs] = mixed
        y_b = _dot_f32_lhs(mx_buf[...], wpb_ref[...])

        gate_a = _sigmoid(gate_a_pre)
        gate_b = _sigmoid(gate_b_pre)
        merged = gate_a * y_a + gate_b * y_b
        mix = _dot_f32_lhs(merged, wout_ref[...])
        h1 = _layer_norm(ALPHA * h0 + mix, ln1g_ref[...], ln1b_ref[...])
        h1_ref[bi] = h1
        if not emit_state:
            ht_ref[bi] = h1.T.astype(_BF16)

        xr_buf[0:CONV_TAIL, :] = xr_buf[tt:tt + CONV_TAIL, :]
        xp_buf[0:POOL_TAIL, :] = xp_buf[tt:tt + POOL_TAIL, :]
        if emit_state:
            xr_out[...] = xr_buf[0:CONV_TAIL, :]
            xp_out[...] = xp_buf[0:POOL_TAIL, :]
            hs_out[...] = hs_buf[...]

    return tail


def _const_spec(shape):
    nd = len(shape)
    return pl.BlockSpec(shape, lambda b, t: (0,) * nd, pipeline_mode=pl.Buffered(1))


def _mixer(x, weights, state, *, tt, pos0, emit_state, nb=1):
    B, S, D = x.shape
    assert S % tt == 0 and tt % POOL_TAIL == 0 and B % nb == 0
    in_specs = [pl.BlockSpec((nb, tt, D), lambda b, t: (b, t, 0))]
    in_specs += [_const_spec(w.shape) for w in weights]
    in_specs += [_const_spec(s.shape) for s in state]
    out_shape = [jax.ShapeDtypeStruct((B, S, D), _F32)]
    out_specs = [pl.BlockSpec((nb, tt, D), lambda b, t: (b, t, 0))]
    if emit_state:
        assert B == 1 and S == tt
        out_shape += [jax.ShapeDtypeStruct(s.shape, _F32) for s in state]
        out_specs += [pl.BlockSpec(s.shape, lambda b, t: (0, 0)) for s in state]
    else:
        assert tt % LANES == 0
        out_shape += [jax.ShapeDtypeStruct((B, D, S), _BF16)]
        out_specs += [pl.BlockSpec((nb, D, tt), lambda b, t: (b, 0, t))]
    scratch = [
        pltpu.VMEM((nb, CONV_TAIL + tt, D_RNN), _F32),
        pltpu.VMEM((nb, POOL_TAIL + tt, D_POOL), _F32),
        pltpu.VMEM((nb, SUBLANES, D_RNN), _F32),
        pltpu.VMEM((nb, tt, D_RNN), _F32),
        pltpu.VMEM((nb, tt, D_POOL), _F32),
    ]
    return pl.pallas_call(
        functools.partial(_mixer_kernel, tt=tt, pos0=pos0, emit_state=emit_state, nb=nb),
        grid=(B // nb, S // tt),
        in_specs=in_specs,
        out_specs=out_specs,
        out_shape=out_shape,
        scratch_shapes=scratch,
        compiler_params=pltpu.CompilerParams(
            dimension_semantics=("arbitrary", "arbitrary"), vmem_limit_bytes=VMEM_LIMIT),
        name="mixer_meta" if emit_state else "mixer",
    )(x, *weights, *state)


def _oddeven_merge_sort_pairs(n):
    pairs = []

    def merge(lo, hi, r):
        step = r * 2
        if step < hi - lo:
            merge(lo, hi, step)
            merge(lo + r, hi, step)
            for i in range(lo + r, hi - r, step):
                pairs.append((i, i + r))
        else:
            pairs.append((lo, lo + r))

    def sort(lo, hi):
        if hi - lo >= 1:
            mid = lo + (hi - lo) // 2
            sort(lo, mid)
            sort(mid + 1, hi)
            merge(lo, hi, 1)

    sort(0, n - 1)
    return pairs


_SORT16 = _oddeven_merge_sort_pairs(PEER_TOPK)


def _cmpx(v, i, j):
    hi = jnp.maximum(v[i], v[j])
    lo = jnp.minimum(v[i], v[j])
    v[i] = hi
    v[j] = lo


def _sort16_desc(v):
    v = list(v)
    for i, j in _SORT16:
        _cmpx(v, i, j)
    return v


def _merge_top16(a, b):
    c = list(a)
    for k in range(len(b)):
        c[PEER_TOPK - 1 - k] = jnp.maximum(a[PEER_TOPK - 1 - k], b[k])
    stride = PEER_TOPK // 2
    while stride >= 1:
        for i in range(PEER_TOPK):
            if i & stride == 0:
                _cmpx(c, i, i + stride)
        stride //= 2
    return c


def _top16_desc(vals):
    runs = [_sort16_desc(vals[i:i + PEER_TOPK]) for i in range(0, len(vals), PEER_TOPK)]
    while len(runs) > 1:
        runs = [_merge_top16(runs[i], runs[i + 1]) for i in range(0, len(runs), 2)]
    return runs[0]


_CAND = [(i, j) for i in range(PEER_TOPK) for j in range(PEER_TOPK) if (i + 1) * (j + 1) <= PEER_TOPK]


def _prefix_count(vals, pred):
    sel = jnp.where
    m8 = pred(vals[7])
    m4 = pred(sel(m8, vals[11], vals[3]))
    m2 = pred(sel(m8, sel(m4, vals[13], vals[9]), sel(m4, vals[5], vals[1])))
    lo = sel(m4, sel(m2, vals[6], vals[4]), sel(m2, vals[2], vals[0]))
    hi = sel(m4, sel(m2, vals[14], vals[12]), sel(m2, vals[10], vals[8]))
    m1 = pred(sel(m8, hi, lo))
    cnt = sel(m8, 8.0, 0.0) + sel(m4, 4.0, 0.0) + sel(m2, 2.0, 0.0) + sel(m1, 1.0, 0.0)
    return sel(pred(vals[15]), 16.0, cnt)


def _dup_bf16_words(v):
    bits = lax.bitcast_convert_type(v.astype(_BF16).astype(_F32), jnp.uint32)
    return bits | (bits >> 16)


def _retrieve_kernel(ht_ref, htn_ref, wq_ref, keys_ref, cnt1_ref, e1_ref, rho2_ref, e2_ref,
                     sa_buf, sb_buf, q_buf, x_buf, *, tn, chunk):
    step = pl.program_id(0)
    half_rows = PEER_HEADS * PEER_HALF

    def mm_piece(c, ht, s_dst):
        if c < 2:
            rs = slice(c * half_rows, (c + 1) * half_rows)
            q_buf[rs, :] = _dot_f32_lhs(wq_ref[rs, :], ht[0]).astype(_BF16)
        else:
            p = c - 2
            s_dst[p] = _dot_f32_lhs(keys_ref[p], q_buf[p * half_rows:(p + 1) * half_rows, :])

    @pl.when(step == 0)
    def _():
        for c in range(4):
            mm_piece(c, ht_ref, sa_buf)

    neg = jnp.full((PEER_HEADS, chunk), -jnp.inf, _F32)

    def slab(k):
        return pl.ds(pl.multiple_of(k * PEER_HEADS, PEER_HEADS), PEER_HEADS)

    def rank_chunk(c, s_buf, s_nxt):
        ls = slice(c * chunk, (c + 1) * chunk)
        a = _top16_desc([s_buf[0, k * PEER_HEADS:(k + 1) * PEER_HEADS, ls] for k in range(PEER_NKEYS)])
        b = _top16_desc([s_buf[1, k * PEER_HEADS:(k + 1) * PEER_HEADS, ls] for k in range(PEER_NKEYS)])
        rows = {}
        for i, j in _CAND:
            rows.setdefault(i, []).append(a[i] + b[j])
        top = rows[0]
        rest = [v for i in range(1, PEER_TOPK) for v in rows[i]]
        while rest:
            grp, rest = rest[:PEER_TOPK], rest[PEER_TOPK:]
            grp = grp + [neg] * (PEER_TOPK - len(grp))
            top = _merge_top16(top, _sort16_desc(grp))
        c16 = top[PEER_TOPK - 1]
        ea = [jnp.exp(a[i] - a[0]) for i in range(PEER_TOPK)]
        eb = [jnp.exp(b[j] - b[0]) for j in range(PEER_TOPK)]
        z = jnp.zeros((PEER_HEADS, chunk), _F32)
        for i, j in _CAND:
            z = z + jnp.where(a[i] + b[j] >= c16, ea[i] * eb[j], 0.0)
        zinv = 1.0 / z

        mm_piece(c, htn_ref, s_nxt)

        def key_body(kk, _):
            for u in range(KEY_UNROLL):
                k = kk * KEY_UNROLL + u
                s1 = s_buf[0, slab(k), ls]
                s2 = s_buf[1, slab(k), ls]
                cnt = _prefix_count(b, lambda v: s1 + v >= c16)
                rho = _prefix_count(b, lambda v: v > s2)
                x_buf[0, slab(k), :] = lax.bitcast_convert_type(_dup_bf16_words(cnt), _F32)
                x_buf[1, slab(k), :] = lax.bitcast_convert_type(
                    _dup_bf16_words(jnp.exp(s1 - a[0]) * zinv), _F32)
                x_buf[2, slab(k), :] = rho
                x_buf[3, slab(k), :] = jnp.exp(s2 - b[0])
            return 0

        lax.fori_loop(0, PEER_NKEYS // KEY_UNROLL, key_body, 0)

        for h in range(PEER_HEADS):
            rows_h = pl.ds(h, PEER_NKEYS, stride=PEER_HEADS)
            cnt1_ref[h, :, ls] = lax.bitcast_convert_type(x_buf[0, rows_h, :], jnp.uint32)
            e1_ref[h, :, ls] = lax.bitcast_convert_type(x_buf[1, rows_h, :], jnp.uint32)
            rho2_ref[h, :, ls] = x_buf[2, rows_h, :].astype(_BF16)
            e2_ref[h, :, ls] = x_buf[3, rows_h, :].astype(_BF16)

    @pl.when(step % 2 == 0)
    def _():
        for c in range(tn // chunk):
            rank_chunk(c, sa_buf, sb_buf)

    @pl.when(step % 2 == 1)
    def _():
        for c in range(tn // chunk):
            rank_chunk(c, sb_buf, sa_buf)


def _retrieve(h_t, wq_t, keys_kh, *, tn, chunk=LANES):
    B, _, S = h_t.shape
    N = B * S
    assert S % tn == 0 and tn == 4 * chunk and chunk == LANES
    n_t = N // tn
    nst = S // tn
    ht_spec = lambda f: pl.BlockSpec((1, D_MODEL, tn), lambda i: (f(i) // nst, 0, f(i) % nst))
    sds = lambda dt: jax.ShapeDtypeStruct((PEER_HEADS, PEER_NKEYS, N), dt)
    out_spec = pl.BlockSpec((PEER_HEADS, PEER_NKEYS, tn), lambda i: (0, 0, i))
    s_shape = (2, PEER_NKEYS * PEER_HEADS, tn)
    return pl.pallas_call(
        functools.partial(_retrieve_kernel, tn=tn, chunk=chunk),
        grid=(n_t,),
        in_specs=[
            ht_spec(lambda i: i),
            ht_spec(lambda i: jnp.minimum(i + 1, n_t - 1)),
            pl.BlockSpec(wq_t.shape, lambda i: (0, 0), pipeline_mode=pl.Buffered(1)),
            pl.BlockSpec(keys_kh.shape, lambda i: (0, 0, 0), pipeline_mode=pl.Buffered(1)),
        ],
        out_specs=[out_spec] * 4,
        out_shape=[sds(jnp.uint32), sds(jnp.uint32), sds(_BF16), sds(_BF16)],
        scratch_shapes=[
            pltpu.VMEM(s_shape, _F32),
            pltpu.VMEM(s_shape, _F32),
            pltpu.VMEM((2 * PEER_HEADS * PEER_HALF, tn), _BF16),
            pltpu.VMEM((4, PEER_NKEYS * PEER_HEADS, chunk), _F32),
        ],
        compiler_params=pltpu.CompilerParams(
            dimension_semantics=("arbitrary",), vmem_limit_bytes=VMEM_LIMIT),
        name="peer_retrieve",
    )(h_t, h_t, wq_t, keys_kh)


def _experts_kernel(ht_ref, h1_ref, u_ref, vt_ref, cnt1_ref, e1_ref, rho2_ref, e2_ref, g_ref, b_ref,
                    out_ref, z0_buf, z1_buf, gz0_buf, gz1_buf, acc_buf, *, tn, te, n_e, lane_chunk):
    j = pl.program_id(0)
    rows = te // PEER_NKEYS
    e_prev = (j + n_e - 1) % n_e
    e_pp = (j + n_e - 2) % n_e

    @pl.when(j == 0)
    def _():
        z1_buf[...] = jnp.zeros_like(z1_buf)
        gz0_buf[...] = jnp.zeros_like(gz0_buf)
        gz1_buf[...] = jnp.zeros_like(gz1_buf)

    @pl.when(jnp.logical_or(e_pp == 0, j == 0))
    def _():
        acc_buf[...] = jnp.zeros_like(acc_buf)

    def stages(z_new, z_cur, gz_cur, gz_old):
        k1_base = pl.multiple_of(e_prev * rows, rows)
        zero = jnp.zeros((PACK, lane_chunk), _BF16)

        def weights(r_lo, r_hi):
            for c in range(tn // lane_chunk):
                ls = slice(c * lane_chunk, (c + 1) * lane_chunk)
                cnt1 = [cnt1_ref[h, pl.ds(k1_base, rows), ls] for h in range(PEER_HEADS)]
                e1 = [e1_ref[h, pl.ds(k1_base, rows), ls] for h in range(PEER_HEADS)]
                for r in range(r_lo, r_hi):
                    def bcast(w):
                        w8 = jnp.broadcast_to(w[r:r + 1, :], (SUBLANES, lane_chunk))
                        return pltpu.bitcast(w8, _BF16)
                    n_q = PEER_NKEYS // PACK
                    g = [None] * n_q
                    for h in range(PEER_HEADS):
                        cb = bcast(cnt1[h])
                        eb = bcast(e1[h])
                        for q in range(n_q):
                            ks = slice(q * PACK, (q + 1) * PACK)
                            m = rho2_ref[h, ks, ls] < cb
                            term = jnp.where(m, e2_ref[h, ks, ls], zero) * eb
                            g[q] = term if g[q] is None else g[q] + term
                    for q in range(n_q):
                        rs = slice(r * PEER_NKEYS + q * PACK, r * PEER_NKEYS + (q + 1) * PACK)
                        zz = z_cur[rs, ls].astype(_BF16)
                        gz_cur[rs, ls] = g[q] * _gelu_sig(zz)

        weights(0, rows)
        z_new[...] = _dot_f32_lhs(u_ref[...], ht_ref[0])
        acc_buf[...] += _dot_f32_lhs(vt_ref[...], gz_old[...])

    @pl.when(j % 2 == 0)
    def _():
        stages(z0_buf, z1_buf, gz1_buf, gz0_buf)

    @pl.when(j % 2 == 1)
    def _():
        stages(z1_buf, z0_buf, gz0_buf, gz1_buf)

    @pl.when(jnp.logical_and(e_pp == n_e - 1, j >= 2))
    def _():
        ffn = acc_buf[...].T
        out_ref[...] = _layer_norm(ALPHA * h1_ref[...] + ffn, g_ref[...], b_ref[...])


def _experts(h_t, h1, u_b, vt_b, cnt1, e1, rho2, e2, ln_g, ln_b, *, tn, te, lane_chunk=256):
    N = h1.shape[0]
    nst = h_t.shape[2] // tn
    assert h_t.shape[2] % tn == 0 and N_EXPERTS % te == 0 and te % (SUBLANES * PEER_NKEYS) == 0
    n_e = N_EXPERTS // te
    total = (N // tn) * n_e
    last = total - 1
    tile0 = lambda j: jnp.minimum(j, last)
    tile1 = lambda j: jnp.clip(j - 1, 0, last)
    tile2 = lambda j: jnp.clip(j - 2, 0, last)
    sel_spec = pl.BlockSpec((PEER_HEADS, PEER_NKEYS, tn), lambda j: (0, 0, tile1(j) // n_e))
    return pl.pallas_call(
        functools.partial(_experts_kernel, tn=tn, te=te, n_e=n_e, lane_chunk=min(lane_chunk, tn)),
        grid=(total + 2,),
        in_specs=[
            pl.BlockSpec((1, D_MODEL, tn),
                         lambda j: ((tile0(j) // n_e) // nst, 0, (tile0(j) // n_e) % nst)),
            pl.BlockSpec((tn, D_MODEL), lambda j: (tile2(j) // n_e, 0)),
            pl.BlockSpec((te, D_MODEL), lambda j: (tile0(j) % n_e, 0)),
            pl.BlockSpec((D_MODEL, te), lambda j: (0, tile2(j) % n_e)),
            sel_spec, sel_spec, sel_spec, sel_spec,
            pl.BlockSpec((1, D_MODEL), lambda j: (0, 0)),
            pl.BlockSpec((1, D_MODEL), lambda j: (0, 0)),
        ],
        out_specs=pl.BlockSpec((tn, D_MODEL), lambda j: (tile2(j) // n_e, 0)),
        out_shape=jax.ShapeDtypeStruct((N, D_MODEL), _F32),
        scratch_shapes=[
            pltpu.VMEM((te, tn), _F32),
            pltpu.VMEM((te, tn), _F32),
            pltpu.VMEM((te, tn), _BF16),
            pltpu.VMEM((te, tn), _BF16),
            pltpu.VMEM((D_MODEL, tn), _F32),
        ],
        compiler_params=pltpu.CompilerParams(
            dimension_semantics=("arbitrary",), vmem_limit_bytes=VMEM_LIMIT),
        name="peer_experts",
    )(h_t, h1, u_b, vt_b, cnt1, e1, rho2, e2, ln_g, ln_b)


def _pick_tile(n, target):
    t = min(n, target)
    while n % t:
        t //= 2
    return t


def kernel(x, meta, ln_in_g, ln_in_b, w_in, conv_w, conv_b, w_rg, b_rg, w_ig, b_ig, lru_L, w_proj_a, pool_w, pool_scale, w_proj_b, w_out, ln1_g, ln1_b, w_q, sub_keys, expert_u, expert_v, ln2_g, ln2_b):
    B, S, D = x.shape
    assert D == D_MODEL and w_in.shape[0] == DEPTH
    row = lambda v: v.reshape(1, -1).astype(_F32)
    l = 0
    mixer_weights = (
        row(ln_in_g), row(ln_in_b), w_in[l].astype(_BF16), conv_w[l].astype(_F32), row(conv_b[l]),
        jnp.concatenate([w_rg[l], w_ig[l]], axis=-1).astype(_BF16), row(b_rg[l]), row(b_ig[l]),
        row(lru_L[l]), w_proj_a[l].astype(_BF16), pool_w[l].astype(_BF16), row(pool_scale[l]),
        w_proj_b[l].astype(_BF16), w_out[l].astype(_BF16), row(ln1_g[l]), row(ln1_b[l]),
    )
    zero_state = (jnp.zeros((CONV_TAIL, D_RNN), _F32), jnp.zeros((POOL_TAIL, D_POOL), _F32),
                  jnp.zeros((SUBLANES, D_RNN), _F32))
    _, xr_t, xp_t, hs = _mixer(meta[None].astype(_F32), mixer_weights, zero_state,
                               tt=N_META, pos0=0, emit_state=True)
    h1, h_t = _mixer(x, mixer_weights, (xr_t, xp_t, hs), tt=_pick_tile(S, 256), pos0=N_META,
                     emit_state=False, nb=2 if B % 2 == 0 else 1)
    h1 = h1.reshape(B * S, D)
    N = B * S

    wq_t = w_q[l].reshape(D, PEER_HEADS, 2, PEER_HALF).transpose(2, 1, 3, 0)
    wq_t = wq_t.reshape(2 * PEER_HEADS * PEER_HALF, D)
    eye = jnp.eye(PEER_HEADS, dtype=_F32)
    keys_kh = jnp.einsum("hpkd,hg->pkhgd", sub_keys[l], eye)
    keys_kh = keys_kh.reshape(2, PEER_NKEYS * PEER_HEADS, PEER_HEADS * PEER_HALF)
    tn = _pick_tile(N, 512)
    cnt1, e1, rho2, e2 = _retrieve(h_t, wq_t, keys_kh, tn=tn)

    u_b = expert_u[l]
    vt_b = expert_v[l].T
    out = _experts(h_t, h1, u_b, vt_b, cnt1, e1, rho2, e2, row(ln2_g[l]), row(ln2_b[l]),
                   tn=tn, te=1024)
    return out.reshape(B, S, D).astype(x.dtype)
```

```python
import functools

import jax
import jax.numpy as jnp
from jax import lax
from jax.experimental import pallas as pl
from jax.experimental.pallas import tpu as pltpu

D_MODEL = 1024
N_META = 16
D_RNN = D_MODEL
N_RNN_BLOCKS = 4
RNN_BLOCK = D_RNN // N_RNN_BLOCKS
CONV_WIDTH = 4
LRU_C = 8.0
POOL_WINDOWS = (2, 4, 8, 16)
D_POOL = D_MODEL // 2
POOL_GROUP = D_POOL // len(POOL_WINDOWS)
PEER_HEADS = 8
PEER_NKEYS = 128
N_EXPERTS = PEER_NKEYS * PEER_NKEYS
PEER_HALF = 128
PEER_TOPK = 16
DEPTH = 1
ALPHA = (2.0 * DEPTH) ** 0.25
LN_EPS = 1e-5

SUBLANES = 8
PACK = 16
KEY_UNROLL = 8
LANES = 128
CONV_TAIL = SUBLANES
POOL_TAIL = 16
VMEM_LIMIT = 56 * 1024 * 1024

_F32 = jnp.float32
_BF16 = jnp.bfloat16


def _layer_norm(v, g, b):
    mu = jnp.mean(v, axis=-1, keepdims=True)
    vc = v - mu
    var = jnp.mean(vc * vc, axis=-1, keepdims=True)
    return vc * lax.rsqrt(var + LN_EPS) * g + b


def _gelu_sig(v):
    w = v * (-2.3022082870680315 - 0.10294324120074478 * (v * v))
    return v / (1.0 + jnp.exp2(w))


def _sigmoid(v):
    return 1.0 / (1.0 + jnp.exp(-v))


def _dot_f32_lhs(a, b):
    return lax.dot_general(a, b, (((1,), (0,)), ((), ())), preferred_element_type=_F32)


def _lru_scan(a, b, h_prev):
    tt, c = a.shape
    groups = tt // SUBLANES
    a3 = a.reshape(groups, SUBLANES, c)
    b3 = b.reshape(groups, SUBLANES, c)
    row = lax.broadcasted_iota(jnp.int32, a3.shape, 1)
    shift = 1
    while shift < SUBLANES:
        a_sh = pltpu.roll(a3, shift, axis=1)
        b_sh = pltpu.roll(b3, shift, axis=1)
        m = row >= shift
        b3 = jnp.where(m, a3 * b_sh + b3, b3)
        a3 = jnp.where(m, a3 * a_sh, a3)
        shift *= 2
    outs = []
    carry = h_prev
    for g in range(groups):
        hg = a3[g] * carry + b3[g]
        outs.append(hg)
        carry = hg[SUBLANES - 1:SUBLANES, :]
    return jnp.concatenate(outs, axis=0)


def _mixer_kernel(x_ref, lng_ref, lnb_ref, win_ref, convw_ref, convb_ref, wgate_ref, brg_ref,
                  big_ref, lrul_ref, wpa_ref, poolw_ref, pscale_ref, wpb_ref, wout_ref,
                  ln1g_ref, ln1b_ref, xr0_ref, xp0_ref, hs0_ref,
                  h1_ref, *rest, tt, pos0, emit_state, nb):
    xr_all, xp_all, hs_all = rest[-5:-2]

    @pl.when(pl.program_id(1) == 0)
    def _():
        for bi in range(nb):
            xr_all[bi, 0:CONV_TAIL, :] = xr0_ref[...]
            xp_all[bi, 0:POOL_TAIL, :] = xp0_ref[...]
            hs_all[bi] = hs0_ref[...]

    tails = [
        _mixer_tile(bi, x_ref, lng_ref, lnb_ref, win_ref, convw_ref, convb_ref, wgate_ref, brg_ref,
                    big_ref, lrul_ref, wpa_ref, poolw_ref, pscale_ref, wpb_ref, wout_ref,
                    ln1g_ref, ln1b_ref, h1_ref, rest, tt=tt, pos0=pos0, emit_state=emit_state)
        for bi in range(nb)]
    for tail in tails:
        tail()


def _mixer_tile(bi, x_ref, lng_ref, lnb_ref, win_ref, convw_ref, convb_ref, wgate_ref, brg_ref,
                big_ref, lrul_ref, wpa_ref, poolw_ref, pscale_ref, wpb_ref, wout_ref,
                ln1g_ref, ln1b_ref, h1_ref, rest, *, tt, pos0, emit_state):
    if emit_state:
        xr_out, xp_out, hs_out, xr_all, xp_all, hs_all, rg_all, mx_all = rest
    else:
        ht_ref, xr_all, xp_all, hs_all, rg_all, mx_all = rest
    xr_buf, xp_buf, hs_buf, rg_buf, mx_buf = (r.at[bi] for r in (xr_all, xp_all, hs_all, rg_all, mx_all))
    t = pl.program_id(1)
    x = x_ref[bi]
    h0 = _layer_norm(x, lng_ref[...], lnb_ref[...])

    xr_buf[CONV_TAIL:CONV_TAIL + tt, :] = _dot_f32_lhs(h0, win_ref[:, 0:D_RNN])
    lsig = lrul_ref[...]
    log_sig = jnp.minimum(lsig, 0.0) - jnp.log1p(jnp.exp(-jnp.abs(lsig)))
    for n in range(N_RNN_BLOCKS):
        cs = slice(n * RNN_BLOCK, (n + 1) * RNN_BLOCK)
        xc = convb_ref[:, cs]
        for j in range(CONV_WIDTH):
            off = CONV_TAIL - (CONV_WIDTH - 1) + j
            xc = xc + convw_ref[j:j + 1, cs] * xr_buf[off:off + tt, cs]
        pre = _dot_f32_lhs(xc, wgate_ref[n])
        r = _sigmoid(pre[:, :RNN_BLOCK] + brg_ref[:, cs])
        i = _sigmoid(pre[:, RNN_BLOCK:] + big_ref[:, cs])
        log_a = (LRU_C * log_sig[:, cs]) * r
        a = jnp.exp(log_a)
        mult = jnp.sqrt(1.0 - a * a)
        h = _lru_scan(a, mult * i * xc, hs_buf[0:1, cs])
        hs_buf[:, cs] = jnp.broadcast_to(h[tt - 1:tt, :], (SUBLANES, RNN_BLOCK))
        gate = _dot_f32_lhs(h0, win_ref[:, D_RNN + n * RNN_BLOCK:D_RNN + (n + 1) * RNN_BLOCK])
        rg_buf[:, cs] = h * _gelu_sig(gate)
    xp_buf[POOL_TAIL:POOL_TAIL + tt, :] = _dot_f32_lhs(h0, win_ref[:, 2 * D_RNN:2 * D_RNN + D_POOL])
    g_off = 2 * D_RNN + D_POOL
    gate_a_pre = _dot_f32_lhs(h0, win_ref[:, g_off:g_off + D_MODEL])
    gate_b_pre = _dot_f32_lhs(h0, win_ref[:, g_off + D_MODEL:g_off + 2 * D_MODEL])

    def tail():
        y_a = _dot_f32_lhs(rg_buf[...], wpa_ref[...])

        if pos0 < max(POOL_WINDOWS):
            pos = (lax.broadcasted_iota(jnp.int32, (tt, 1), 0) + (pos0 + 1) + t * tt).astype(_F32)
        for g, w in enumerate(POOL_WINDOWS):
            cs = slice(g * POOL_GROUP, (g + 1) * POOL_GROUP)
            cur = xp_buf[POOL_TAIL:POOL_TAIL + tt, cs]
            acc = cur
            for j in range(1, w):
                acc = acc + xp_buf[POOL_TAIL - j:POOL_TAIL - j + tt, cs]
            if pos0 < max(POOL_WINDOWS):
                mean = acc / jnp.minimum(pos, float(w))
            else:
                mean = acc * (1.0 / w)
            mixed = _dot_f32_lhs(mean - cur, poolw_ref[g]) * pscale_ref[:, cs]
            mx_buf[:, cs] = mixed
        y_b = _dot_f32_lhs(mx_buf[...], wpb_ref[...])

        gate_a = _sigmoid(gate_a_pre)
        gate_b = _sigmoid(gate_b_pre)
        merged = gate_a * y_a + gate_b * y_b
        mix = _dot_f32_lhs(merged, wout_ref[...])
        h1 = _layer_norm(ALPHA * h0 + mix, ln1g_ref[...], ln1b_ref[...])
        h1_ref[bi] = h1
        if not emit_state:
            ht_ref[bi] = h1.T.astype(_BF16)

        xr_buf[0:CONV_TAIL, :] = xr_buf[tt:tt + CONV_TAIL, :]
        xp_buf[0:POOL_TAIL, :] = xp_buf[tt:tt + POOL_TAIL, :]
        if emit_state:
            xr_out[...] = xr_buf[0:CONV_TAIL, :]
            xp_out[...] = xp_buf[0:POOL_TAIL, :]
            hs_out[...] = hs_buf[...]

    return tail


def _const_spec(shape):
    nd = len(shape)
    return pl.BlockSpec(shape, lambda b, t: (0,) * nd, pipeline_mode=pl.Buffered(1))


def _mixer(x, weights, state, *, tt, pos0, emit_state, nb=1):
    B, S, D = x.shape
    assert S % tt == 0 and tt % POOL_TAIL == 0 and B % nb == 0
    in_specs = [pl.BlockSpec((nb, tt, D), lambda b, t: (b, t, 0))]
    in_specs += [_const_spec(w.shape) for w in weights]
    in_specs += [_const_spec(s.shape) for s in state]
    out_shape = [jax.ShapeDtypeStruct((B, S, D), _F32)]
    out_specs = [pl.BlockSpec((nb, tt, D), lambda b, t: (b, t, 0))]
    if emit_state:
        assert B == 1 and S == tt
        out_shape += [jax.ShapeDtypeStruct(s.shape, _F32) for s in state]
        out_specs += [pl.BlockSpec(s.shape, lambda b, t: (0, 0)) for s in state]
    else:
        assert tt % LANES == 0
        out_shape += [jax.ShapeDtypeStruct((B, D, S), _BF16)]
        out_specs += [pl.BlockSpec((nb, D, tt), lambda b, t: (b, 0, t))]
    scratch = [
        pltpu.VMEM((nb, CONV_TAIL + tt, D_RNN), _F32),
        pltpu.VMEM((nb, POOL_TAIL + tt, D_POOL), _F32),
        pltpu.VMEM((nb, SUBLANES, D_RNN), _F32),
        pltpu.VMEM((nb, tt, D_RNN), _F32),
        pltpu.VMEM((nb, tt, D_POOL), _F32),
    ]
    return pl.pallas_call(
        functools.partial(_mixer_kernel, tt=tt, pos0=pos0, emit_state=emit_state, nb=nb),
        grid=(B // nb, S // tt),
        in_specs=in_specs,
        out_specs=out_specs,
        out_shape=out_shape,
        scratch_shapes=scratch,
        compiler_params=pltpu.CompilerParams(
            dimension_semantics=("arbitrary", "arbitrary"), vmem_limit_bytes=VMEM_LIMIT),
        name="mixer_meta" if emit_state else "mixer",
    )(x, *weights, *state)


def _oddeven_merge_sort_pairs(n):
    pairs = []

    def merge(lo, hi, r):
        step = r * 2
        if step < hi - lo:
            merge(lo, hi, step)
            merge(lo + r, hi, step)
            for i in range(lo + r, hi - r, step):
                pairs.append((i, i + r))
        else:
            pairs.append((lo, lo + r))

    def sort(lo, hi):
        if hi - lo >= 1:
            mid = lo + (hi - lo) // 2
            sort(lo, mid)
            sort(mid + 1, hi)
            merge(lo, hi, 1)

    sort(0, n - 1)
    return pairs


_SORT16 = _oddeven_merge_sort_pairs(PEER_TOPK)


def _cmpx(v, i, j):
    hi = jnp.maximum(v[i], v[j])
    lo = jnp.minimum(v[i], v[j])
    v[i] = hi
    v[j] = lo


def _sort16_desc(v):
    v = list(v)
    for i, j in _SORT16:
        _cmpx(v, i, j)
    return v


def _merge_top16(a, b):
    c = list(a)
    for k in range(len(b)):
        c[PEER_TOPK - 1 - k] = jnp.maximum(a[PEER_TOPK - 1 - k], b[k])
    stride = PEER_TOPK // 2
    while stride >= 1:
        for i in range(PEER_TOPK):
            if i & stride == 0:
                _cmpx(c, i, i + stride)
        stride //= 2
    return c


def _top16_desc(vals):
    runs = [_sort16_desc(vals[i:i + PEER_TOPK]) for i in range(0, len(vals), PEER_TOPK)]
    while len(runs) > 1:
        runs = [_merge_top16(runs[i], runs[i + 1]) for i in range(0, len(runs), 2)]
    return runs[0]


_CAND = [(i, j) for i in range(PEER_TOPK) for j in range(PEER_TOPK) if (i + 1) * (j + 1) <= PEER_TOPK]


def _prefix_count(vals, pred):
    sel = jnp.where
    m8 = pred(vals[7])
    m4 = pred(sel(m8, vals[11], vals[3]))
    m2 = pred(sel(m8, sel(m4, vals[13], vals[9]), sel(m4, vals[5], vals[1])))
    lo = sel(m4, sel(m2, vals[6], vals[4]), sel(m2, vals[2], vals[0]))
    hi = sel(m4, sel(m2, vals[14], vals[12]), sel(m2, vals[10], vals[8]))
    m1 = pred(sel(m8, hi, lo))
    cnt = sel(m8, 8.0, 0.0) + sel(m4, 4.0, 0.0) + sel(m2, 2.0, 0.0) + sel(m1, 1.0, 0.0)
    return sel(pred(vals[15]), 16.0, cnt)


def _dup_bf16_words(v):
    bits = lax.bitcast_convert_type(v.astype(_BF16).astype(_F32), jnp.uint32)
    return bits | (bits >> 16)


def _retrieve_kernel(ht_ref, htn_ref, wq_ref, keys_ref, cnt1_ref, e1_ref, rho2_ref, e2_ref,
                     sa_buf, sb_buf, q_buf, x_buf, *, tn, chunk):
    step = pl.program_id(0)
    half_rows = PEER_HEADS * PEER_HALF

    def mm_piece(c, ht, s_dst):
        if c < 2:
            rs = slice(c * half_rows, (c + 1) * half_rows)
            q_buf[rs, :] = _dot_f32_lhs(wq_ref[rs, :], ht[0]).astype(_BF16)
        else:
            p = c - 2
            s_dst[p] = _dot_f32_lhs(keys_ref[p], q_buf[p * half_rows:(p + 1) * half_rows, :])

    @pl.when(step == 0)
    def _():
        for c in range(4):
            mm_piece(c, ht_ref, sa_buf)

    neg = jnp.full((PEER_HEADS, chunk), -jnp.inf, _F32)

    def slab(k):
        return pl.ds(pl.multiple_of(k * PEER_HEADS, PEER_HEADS), PEER_HEADS)

    def rank_chunk(c, s_buf, s_nxt):
        ls = slice(c * chunk, (c + 1) * chunk)
        a = _top16_desc([s_buf[0, k * PEER_HEADS:(k + 1) * PEER_HEADS, ls] for k in range(PEER_NKEYS)])
        b = _top16_desc([s_buf[1, k * PEER_HEADS:(k + 1) * PEER_HEADS, ls] for k in range(PEER_NKEYS)])
        rows = {}
        for i, j in _CAND:
            rows.setdefault(i, []).append(a[i] + b[j])
        top = rows[0]
        rest = [v for i in range(1, PEER_TOPK) for v in rows[i]]
        while rest:
            grp, rest = rest[:PEER_TOPK], rest[PEER_TOPK:]
            grp = grp + [neg] * (PEER_TOPK - len(grp))
            top = _merge_top16(top, _sort16_desc(grp))
        c16 = top[PEER_TOPK - 1]
        ea = [jnp.exp(a[i] - a[0]) for i in range(PEER_TOPK)]
        eb = [jnp.exp(b[j] - b[0]) for j in range(PEER_TOPK)]
        z = jnp.zeros((PEER_HEADS, chunk), _F32)
        for i, j in _CAND:
            z = z + jnp.where(a[i] + b[j] >= c16, ea[i] * eb[j], 0.0)
        zinv = 1.0 / z

        mm_piece(c, htn_ref, s_nxt)

        def key_body(kk, _):
            for u in range(KEY_UNROLL):
                k = kk * KEY_UNROLL + u
                s1 = s_buf[0, slab(k), ls]
                s2 = s_buf[1, slab(k), ls]
                cnt = _prefix_count(b, lambda v: s1 + v >= c16)
                rho = _prefix_count(b, lambda v: v > s2)
                x_buf[0, slab(k), :] = lax.bitcast_convert_type(_dup_bf16_words(cnt), _F32)
                x_buf[1, slab(k), :] = lax.bitcast_convert_type(
                    _dup_bf16_words(jnp.exp(s1 - a[0]) * zinv), _F32)
                x_buf[2, slab(k), :] = rho
                x_buf[3, slab(k), :] = jnp.exp(s2 - b[0])
            return 0

        lax.fori_loop(0, PEER_NKEYS // KEY_UNROLL, key_body, 0)

        for h in range(PEER_HEADS):
            rows_h = pl.ds(h, PEER_NKEYS, stride=PEER_HEADS)
            cnt1_ref[h, :, ls] = lax.bitcast_convert_type(x_buf[0, rows_h, :], jnp.uint32)
            e1_ref[h, :, ls] = lax.bitcast_convert_type(x_buf[1, rows_h, :], jnp.uint32)
            rho2_ref[h, :, ls] = x_buf[2, rows_h, :].astype(_BF16)
            e2_ref[h, :, ls] = x_buf[3, rows_h, :].astype(_BF16)

    @pl.when(step % 2 == 0)
    def _():
        for c in range(tn // chunk):
            rank_chunk(c, sa_buf, sb_buf)

    @pl.when(step % 2 == 1)
    def _():
        for c in range(tn // chunk):
            rank_chunk(c, sb_buf, sa_buf)


def _retrieve(h_t, wq_t, keys_kh, *, tn, chunk=LANES):
    B, _, S = h_t.shape
    N = B * S
    assert S % tn == 0 and tn == 4 * chunk and chunk == LANES
    n_t = N // tn
    nst = S // tn
    ht_spec = lambda f: pl.BlockSpec((1, D_MODEL, tn), lambda i: (f(i) // nst, 0, f(i) % nst))
    sds = lambda dt: jax.ShapeDtypeStruct((PEER_HEADS, PEER_NKEYS, N), dt)
    out_spec = pl.BlockSpec((PEER_HEADS, PEER_NKEYS, tn), lambda i: (0, 0, i))
    s_shape = (2, PEER_NKEYS * PEER_HEADS, tn)
    return pl.pallas_call(
        functools.partial(_retrieve_kernel, tn=tn, chunk=chunk),
        grid=(n_t,),
        in_specs=[
            ht_spec(lambda i: i),
            ht_spec(lambda i: jnp.minimum(i + 1, n_t - 1)),
            pl.BlockSpec(wq_t.shape, lambda i: (0, 0), pipeline_mode=pl.Buffered(1)),
            pl.BlockSpec(keys_kh.shape, lambda i: (0, 0, 0), pipeline_mode=pl.Buffered(1)),
        ],
        out_specs=[out_spec] * 4,
        out_shape=[sds(jnp.uint32), sds(jnp.uint32), sds(_BF16), sds(_BF16)],
        scratch_shapes=[
            pltpu.VMEM(s_shape, _F32),
            pltpu.VMEM(s_shape, _F32),
            pltpu.VMEM((2 * PEER_HEADS * PEER_HALF, tn), _BF16),
            pltpu.VMEM((4, PEER_NKEYS * PEER_HEADS, chunk), _F32),
        ],
        compiler_params=pltpu.CompilerParams(
            dimension_semantics=("arbitrary",), vmem_limit_bytes=VMEM_LIMIT),
        name="peer_retrieve",
    )(h_t, h_t, wq_t, keys_kh)


def _experts_kernel(ht_ref, h1_ref, u_ref, vt_ref, cnt1_ref, e1_ref, rho2_ref, e2_ref, g_ref, b_ref,
                    out_ref, z0_buf, z1_buf, gz0_buf, gz1_buf, acc_buf, *, tn, te, n_e, lane_chunk):
    j = pl.program_id(0)
    rows = te // PEER_NKEYS
    e_prev = (j + n_e - 1) % n_e
    e_pp = (j + n_e - 2) % n_e

    @pl.when(j == 0)
    def _():
        z1_buf[...] = jnp.zeros_like(z1_buf)
        gz0_buf[...] = jnp.zeros_like(gz0_buf)
        gz1_buf[...] = jnp.zeros_like(gz1_buf)

    @pl.when(jnp.logical_or(e_pp == 0, j == 0))
    def _():
        acc_buf[...] = jnp.zeros_like(acc_buf)

    def stages(z_new, z_cur, gz_cur, gz_old):
        k1_base = pl.multiple_of(e_prev * rows, rows)
        zero = jnp.zeros((PACK, lane_chunk), _BF16)

        def weights(r_lo, r_hi):
            for c in range(tn // lane_chunk):
                ls = slice(c * lane_chunk, (c + 1) * lane_chunk)
                cnt1 = [cnt1_ref[h, pl.ds(k1_base, rows), ls] for h in range(PEER_HEADS)]
                e1 = [e1_ref[h, pl.ds(k1_base, rows), ls] for h in range(PEER_HEADS)]
                for r in range(r_lo, r_hi):
                    def bcast(w):
                        w8 = jnp.broadcast_to(w[r:r + 1, :], (SUBLANES, lane_chunk))
                        return pltpu.bitcast(w8, _BF16)
                    n_q = PEER_NKEYS // PACK
                    g = [None] * n_q
                    for h in range(PEER_HEADS):
                        cb = bcast(cnt1[h])
                        eb = bcast(e1[h])
                        for q in range(n_q):
                            ks = slice(q * PACK, (q + 1) * PACK)
                            m = rho2_ref[h, ks, ls] < cb
                            term = jnp.where(m, e2_ref[h, ks, ls], zero) * eb
                            g[q] = term if g[q] is None else g[q] + term
                    for q in range(n_q):
                        rs = slice(r * PEER_NKEYS + q * PACK, r * PEER_NKEYS + (q + 1) * PACK)
                        zz = z_cur[rs, ls].astype(_BF16)
                        gz_cur[rs, ls] = g[q] * _gelu_sig(zz)

        weights(0, rows)
        z_new[...] = _dot_f32_lhs(u_ref[...], ht_ref[0])
        acc_buf[...] += _dot_f32_lhs(vt_ref[...], gz_old[...])

    @pl.when(j % 2 == 0)
    def _():
        stages(z0_buf, z1_buf, gz1_buf, gz0_buf)

    @pl.when(j % 2 == 1)
    def _():
        stages(z1_buf, z0_buf, gz0_buf, gz1_buf)

    @pl.when(jnp.logical_and(e_pp == n_e - 1, j >= 2))
    def _():
        ffn = acc_buf[...].T
        out_ref[...] = _layer_norm(ALPHA * h1_ref[...] + ffn, g_ref[...], b_ref[...])


def _experts(h_t, h1, u_b, vt_b, cnt1, e1, rho2, e2, ln_g, ln_b, *, tn, te, lane_chunk=256):
    N = h1.shape[0]
    nst = h_t.shape[2] // tn
    assert h_t.shape[2] % tn == 0 and N_EXPERTS % te == 0 and te % (SUBLANES * PEER_NKEYS) == 0
    n_e = N_EXPERTS // te
    total = (N // tn) * n_e
    last = total - 1
    tile0 = lambda j: jnp.minimum(j, last)
    tile1 = lambda j: jnp.clip(j - 1, 0, last)
    tile2 = lambda j: jnp.clip(j - 2, 0, last)
    sel_spec = pl.BlockSpec((PEER_HEADS, PEER_NKEYS, tn), lambda j: (0, 0, tile1(j) // n_e))
    return pl.pallas_call(
        functools.partial(_experts_kernel, tn=tn, te=te, n_e=n_e, lane_chunk=min(lane_chunk, tn)),
        grid=(total + 2,),
        in_specs=[
            pl.BlockSpec((1, D_MODEL, tn),
                         lambda j: ((tile0(j) // n_e) // nst, 0, (tile0(j) // n_e) % nst)),
            pl.BlockSpec((tn, D_MODEL), lambda j: (tile2(j) // n_e, 0)),
            pl.BlockSpec((te, D_MODEL), lambda j: (tile0(j) % n_e, 0)),
            pl.BlockSpec((D_MODEL, te), lambda j: (0, tile2(j) % n_e)),
            sel_spec, sel_spec, sel_spec, sel_spec,
            pl.BlockSpec((1, D_MODEL), lambda j: (0, 0)),
            pl.BlockSpec((1, D_MODEL), lambda j: (0, 0)),
        ],
        out_specs=pl.BlockSpec((tn, D_MODEL), lambda j: (tile2(j) // n_e, 0)),
        out_shape=jax.ShapeDtypeStruct((N, D_MODEL), _F32),
        scratch_shapes=[
            pltpu.VMEM((te, tn), _F32),
            pltpu.VMEM((te, tn), _F32),
            pltpu.VMEM((te, tn), _BF16),
            pltpu.VMEM((te, tn), _BF16),
            pltpu.VMEM((D_MODEL, tn), _F32),
        ],
        compiler_params=pltpu.CompilerParams(
            dimension_semantics=("arbitrary",), vmem_limit_bytes=VMEM_LIMIT),
        name="peer_experts",
    )(h_t, h1, u_b, vt_b, cnt1, e1, rho2, e2, ln_g, ln_b)


def _pick_tile(n, target):
    t = min(n, target)
    while n % t:
        t //= 2
    return t


def kernel(x, meta, ln_in_g, ln_in_b, w_in, conv_w, conv_b, w_rg, b_rg, w_ig, b_ig, lru_L, w_proj_a, pool_w, pool_scale, w_proj_b, w_out, ln1_g, ln1_b, w_q, sub_keys, expert_u, expert_v, ln2_g, ln2_b):
    B, S, D = x.shape
    assert D == D_MODEL and w_in.shape[0] == DEPTH
    row = lambda v: v.reshape(1, -1).astype(_F32)
    l = 0
    mixer_weights = (
        row(ln_in_g), row(ln_in_b), w_in[l].astype(_BF16), conv_w[l].astype(_F32), row(conv_b[l]),
        jnp.concatenate([w_rg[l], w_ig[l]], axis=-1).astype(_BF16), row(b_rg[l]), row(b_ig[l]),
        row(lru_L[l]), w_proj_a[l].astype(_BF16), pool_w[l].astype(_BF16), row(pool_scale[l]),
        w_proj_b[l].astype(_BF16), w_out[l].astype(_BF16), row(ln1_g[l]), row(ln1_b[l]),
    )
    zero_state = (jnp.zeros((CONV_TAIL, D_RNN), _F32), jnp.zeros((POOL_TAIL, D_POOL), _F32),
                  jnp.zeros((SUBLANES, D_RNN), _F32))
    _, xr_t, xp_t, hs = _mixer(meta[None].astype(_F32), mixer_weights, zero_state,
                               tt=N_META, pos0=0, emit_state=True)
    h1, h_t = _mixer(x, mixer_weights, (xr_t, xp_t, hs), tt=_pick_tile(S, 256), pos0=N_META,
                     emit_state=False, nb=2 if B % 2 == 0 else 1)
    h1 = h1.reshape(B * S, D)
    N = B * S

    wq_t = w_q[l].reshape(D, PEER_HEADS, 2, PEER_HALF).transpose(2, 1, 3, 0)
    wq_t = wq_t.reshape(2 * PEER_HEADS * PEER_HALF, D)
    eye = jnp.eye(PEER_HEADS, dtype=_F32)
    keys_kh = jnp.einsum("hpkd,hg->pkhgd", sub_keys[l], eye)
    keys_kh = keys_kh.reshape(2, PEER_NKEYS * PEER_HEADS, PEER_HEADS * PEER_HALF)
    tn = _pick_tile(N, 512)
    cnt1, e1, rho2, e2 = _retrieve(h_t, wq_t, keys_kh, tn=tn)

    u_b = expert_u[l]
    vt_b = expert_v[l].T
    out = _experts(h_t, h1, u_b, vt_b, cnt1, e1, rho2, e2, row(ln2_g[l]), row(ln2_b[l]),
                   tn=tn, te=1024)
    return out.reshape(B, S, D).astype(x.dtype)
```
